```python
import math
import jax, jax.numpy as jnp
from jax import lax
import numpy as np

D_MODEL = 2048
BATCH = 8
SEQ = 2048
DEPTH = 2
DEC_BATCH = 32
DEC_SEQ = 4
PAST_LEN = 8192
PAGE_SIZE = 128

N_HEADS = 8
HEAD_DIM = 64
V_DIM = 2 * HEAD_DIM
ATTN_DIM = N_HEADS * V_DIM
ATTN_SCALE = HEAD_DIM ** -0.5
LAMBDA_INIT = 0.8 - 0.6 * math.exp(-0.3 * 0)
Q_BLOCK = 128
CONV_A_DIM = D_MODEL // 2
CONV_A_WIDTH = 31
CONV_C_DIM = D_MODEL
CONV_C_WIDTH = 3
D_FF = 11 * D_MODEL // 4
W_IN_AB_COLS = 2 * CONV_A_DIM + 3 * ATTN_DIM
RMS_EPS = 1e-6
LN_EPS = 1e-5

kernel_name = "hybrid_conformer_diffattn_shortconv_decode_step"


def _rmsnorm(x, g):
    xf = x.astype(jnp.float32)
    y = xf * lax.rsqrt(jnp.mean(xf * xf, axis=-1, keepdims=True) + RMS_EPS)
    return (y * g.astype(jnp.float32)).astype(x.dtype)


def _layernorm(x, g, b):
    xf = x.astype(jnp.float32)
    mu = jnp.mean(xf, axis=-1, keepdims=True)
    var = jnp.mean(jnp.square(xf - mu), axis=-1, keepdims=True)
    y = (xf - mu) * lax.rsqrt(var + LN_EPS) * g.astype(jnp.float32) + b.astype(jnp.float32)
    return y.astype(x.dtype)


def _swiglu(x, w_gate, w_up, w_down):
    return (jax.nn.silu(x @ w_gate) * (x @ w_up)) @ w_down


def _dwconv_valid(x, w):
    c = x.shape[-1]
    return lax.conv_general_dilated(x, w[:, None, :].astype(x.dtype), window_strides=(1,), padding='VALID',
                                    dimension_numbers=('NWC', 'WIO', 'NWC'), feature_group_count=c)


def _alibi_slopes():
    return jnp.exp2(-8.0 * jnp.arange(1, N_HEADS + 1, dtype=jnp.float32) / N_HEADS)


def _diff_scores(q, k, q_pos, k_pos, slopes):
    s = jnp.einsum('bqhmd,bkhmd->mbhqk', q, k, preferred_element_type=jnp.float32) * ATTN_SCALE
    dist = (q_pos[:, None] - k_pos[None, :]).astype(jnp.float32)
    s = s - slopes[None, None, :, None, None] * dist
    return jnp.where(dist >= 0, s, -jnp.inf)


def _diff_combine(s, lam):
    p = jax.nn.softmax(s, axis=-1)
    return p[0] - lam * p[1]


def _diff_lambda(lam_params):
    lp = lam_params.astype(jnp.float32)
    return jnp.exp(jnp.sum(lp[0] * lp[1])) - jnp.exp(jnp.sum(lp[2] * lp[3])) + LAMBDA_INIT


def _head_out(o, g):
    y = o * lax.rsqrt(jnp.mean(o * o, axis=-1, keepdims=True) + RMS_EPS)
    y = y * g.astype(jnp.float32) * (1.0 - LAMBDA_INIT)
    return y.reshape(o.shape[0], o.shape[1], ATTN_DIM)


def _mixer_ab(xn, conv_prefix, attend, w_in, conv_w, conv_b, ln_g, ln_b, subln_g, w_out):
    b, t, _ = xn.shape
    z = xn @ w_in
    a_val = z[..., :CONV_A_DIM]
    a_gate = z[..., CONV_A_DIM:2 * CONV_A_DIM]
    o = 2 * CONV_A_DIM
    q = z[..., o:o + ATTN_DIM].reshape(b, t, N_HEADS, 2, HEAD_DIM)
    k = z[..., o + ATTN_DIM:o + 2 * ATTN_DIM].reshape(b, t, N_HEADS, 2, HEAD_DIM)
    v = z[..., o + 2 * ATTN_DIM:].reshape(b, t, N_HEADS, V_DIM)
    g = a_val * jax.nn.sigmoid(a_gate)
    g_hist = jnp.concatenate([conv_prefix.astype(g.dtype), g], axis=1)
    a = _dwconv_valid(g_hist, conv_w) + conv_b.astype(g.dtype)
    a = jax.nn.silu(_layernorm(a, ln_g, ln_b))
    att = _head_out(attend(q, k, v), subln_g).astype(xn.dtype)
    out = jnp.concatenate([a, att], axis=-1) @ w_out
    return out, k.reshape(b, t, N_HEADS, 2 * HEAD_DIM), v, g_hist[:, -(CONV_A_WIDTH - 1):]


def _mixer_c(xn, conv_prefix, w_in, conv_w, w_out):
    z = xn @ w_in
    gate_b = z[..., :CONV_C_DIM]
    gate_c = z[..., CONV_C_DIM:2 * CONV_C_DIM]
    h = z[..., 2 * CONV_C_DIM:]
    u = gate_c * h
    u_hist = jnp.concatenate([conv_prefix.astype(u.dtype), u], axis=1)
    y = gate_b * _dwconv_valid(u_hist, conv_w)
    return y @ w_out, u_hist[:, -(CONV_C_WIDTH - 1):]


def setup_inputs(seed: int = 0) -> dict:
    key = jax.random.key(seed)
    ks = jax.random.split(key, 32)
    f32 = jnp.float32
    n_pages = PAST_LEN // PAGE_SIZE
    n_phys = (DEC_BATCH * n_pages * 5) // 4

    def w(k, shape, fan_in):
        return jax.random.normal(k, shape, f32) * (fan_in ** -0.5)

    def gain(k, shape):
        return 1.0 + 0.02 * jax.random.normal(k, shape, f32)

    page_table = jax.random.permutation(ks[6], n_phys)[:DEC_BATCH * n_pages].reshape(DEC_BATCH, n_pages).astype(jnp.int32)
    return {
        "x_prompt": jax.random.normal(ks[0], (BATCH, SEQ, D_MODEL), f32),
        "x_sample": jax.random.normal(ks[1], (DEC_BATCH, DEC_SEQ, D_MODEL), f32),
        "cache_k": jax.random.normal(ks[2], (n_phys, PAGE_SIZE, N_HEADS, 2 * HEAD_DIM), f32),
        "cache_v": jax.random.normal(ks[3], (n_phys, PAGE_SIZE, N_HEADS, V_DIM), f32),
        "state_conv_a": 0.5 * jax.random.normal(ks[4], (DEC_BATCH, CONV_A_WIDTH - 1, CONV_A_DIM), f32),
        "state_conv_c": jax.random.normal(ks[5], (DEC_BATCH, CONV_C_WIDTH - 1, CONV_C_DIM), f32),
        "page_table": page_table,
        "norm_ffn1": gain(ks[7], (DEPTH, D_MODEL)),
        "ffn1_w_gate": w(ks[8], (DEPTH, D_MODEL, D_FF), D_MODEL),
        "ffn1_w_up": w(ks[9], (DEPTH, D_MODEL, D_FF), D_MODEL),
        "ffn1_w_down": w(ks[10], (DEPTH, D_FF, D_MODEL), D_FF),
        "norm_mix": gain(ks[11], (DEPTH, D_MODEL)),
        "norm_ffn2": gain(ks[12], (DEPTH, D_MODEL)),
        "ffn2_w_gate": w(ks[13], (DEPTH, D_MODEL, D_FF), D_MODEL),
        "ffn2_w_up": w(ks[14], (DEPTH, D_MODEL, D_FF), D_MODEL),
        "ffn2_w_down": w(ks[15], (DEPTH, D_FF, D_MODEL), D_FF),
        "w_in_ab": w(ks[16], (D_MODEL, W_IN_AB_COLS), D_MODEL),
        "conv_a_w": w(ks[17], (CONV_A_WIDTH, CONV_A_DIM), CONV_A_WIDTH),
        "conv_a_b": 0.02 * jax.random.normal(ks[18], (CONV_A_DIM,), f32),
        "conv_a_ln_g": gain(ks[19], (CONV_A_DIM,)),
        "conv_a_ln_b": 0.02 * jax.random.normal(ks[20], (CONV_A_DIM,), f32),
        "diff_lambda": 0.1 * jax.random.normal(ks[21], (4, HEAD_DIM), f32),
        "diff_subln_g": gain(ks[22], (V_DIM,)),
        "w_out_ab": w(ks[23], (CONV_A_DIM + ATTN_DIM, D_MODEL), CONV_A_DIM + ATTN_DIM),
        "w_in_c": w(ks[24], (D_MODEL, 3 * CONV_C_DIM), D_MODEL),
        "conv_c_w": w(ks[25], (CONV_C_WIDTH, CONV_C_DIM), CONV_C_WIDTH),
        "w_out_c": w(ks[26], (CONV_C_DIM, D_MODEL), CONV_C_DIM),
        "final_norm": gain(ks[27], (D_MODEL,)),
    }


def reference(x_prompt, x_sample, cache_k, cache_v, state_conv_a, state_conv_c, page_table,
              norm_ffn1, ffn1_w_gate, ffn1_w_up, ffn1_w_down, norm_mix, norm_ffn2,
              ffn2_w_gate, ffn2_w_up, ffn2_w_down, w_in_ab, conv_a_w, conv_a_b, conv_a_ln_g,
              conv_a_ln_b, diff_lambda, diff_subln_g, w_out_ab, w_in_c, conv_c_w, w_out_c, final_norm):
    slopes = _alibi_slopes()
    lam = _diff_lambda(diff_lambda)

    def attend_prompt(q, k, v):
        b, t = q.shape[0], q.shape[1]
        nb = t // Q_BLOCK
        k_pos = jnp.arange(t, dtype=jnp.int32)
        q_blocks = jnp.moveaxis(q.reshape(b, nb, Q_BLOCK, N_HEADS, 2, HEAD_DIM), 1, 0)

        def one_block(args):
            q_blk, i = args
            q_pos = i * Q_BLOCK + jnp.arange(Q_BLOCK, dtype=jnp.int32)
            a = _diff_combine(_diff_scores(q_blk, k, q_pos, k_pos, slopes), lam)
            return jnp.einsum('bhqk,bkhd->bqhd', a.astype(v.dtype), v, preferred_element_type=jnp.float32)

        o = lax.map(one_block, (q_blocks, jnp.arange(nb, dtype=jnp.int32)))
        return jnp.moveaxis(o, 0, 1).reshape(b, t, N_HEADS, V_DIM)

    def attend_sample(q, k, v):
        b, s = q.shape[0], q.shape[1]
        past = page_table.shape[1] * cache_k.shape[1]
        k_past = cache_k[page_table].reshape(b, past, N_HEADS, 2, HEAD_DIM)
        v_past = cache_v[page_table].reshape(b, past, N_HEADS, V_DIM)
        q_pos = past + jnp.arange(s, dtype=jnp.int32)
        scores = jnp.concatenate([
            _diff_scores(q, k_past, q_pos, jnp.arange(past, dtype=jnp.int32), slopes),
            _diff_scores(q, k, q_pos, q_pos, slopes)], axis=-1)
        a = _diff_combine(scores, lam)
        o_past = jnp.einsum('bhqk,bkhd->bqhd', a[..., :past].astype(v_past.dtype), v_past, preferred_element_type=jnp.float32)
        o_new = jnp.einsum('bhqk,bkhd->bqhd', a[..., past:].astype(v.dtype), v, preferred_element_type=jnp.float32)
        return o_past + o_new

    def forward(x, attend, conv_a_prefix, conv_c_prefix):
        k_rows = v_rows = conv_a_state = conv_c_state = None
        for layer in range(DEPTH):
            x = x + 0.5 * _swiglu(_rmsnorm(x, norm_ffn1[layer]), ffn1_w_gate[layer], ffn1_w_up[layer], ffn1_w_down[layer])
            xn = _rmsnorm(x, norm_mix[layer])
            if layer % 2 == 0:
                m, k_rows, v_rows, conv_a_state = _mixer_ab(xn, conv_a_prefix, attend, w_in_ab, conv_a_w, conv_a_b,
                                                            conv_a_ln_g, conv_a_ln_b, diff_subln_g, w_out_ab)
            else:
                m, conv_c_state = _mixer_c(xn, conv_c_prefix, w_in_c, conv_c_w, w_out_c)
            x = x + m
            x = x + 0.5 * _swiglu(_rmsnorm(x, norm_ffn2[layer]), ffn2_w_gate[layer], ffn2_w_up[layer], ffn2_w_down[layer])
        return _rmsnorm(x, final_norm), k_rows, v_rows, conv_a_state, conv_c_state

    bp = x_prompt.shape[0]
    zeros_a = jnp.zeros((bp, CONV_A_WIDTH - 1, CONV_A_DIM), x_prompt.dtype)
    zeros_c = jnp.zeros((bp, CONV_C_WIDTH - 1, CONV_C_DIM), x_prompt.dtype)
    y_prompt, k_prompt, v_prompt, conv_a_prompt, conv_c_prompt = forward(x_prompt, attend_prompt, zeros_a, zeros_c)
    y_sample, k_sample, v_sample, conv_a_sample, conv_c_sample = forward(x_sample, attend_sample, state_conv_a, state_conv_c)
    return (y_prompt, y_sample, k_prompt, v_prompt, conv_a_prompt, conv_c_prompt,
            k_sample, v_sample, conv_a_sample, conv_c_sample)
```

```python
import functools
import math

import jax
import jax.numpy as jnp
from jax import lax
from jax.experimental import pallas as pl
from jax.experimental.pallas import tpu as pltpu

F32 = jnp.float32
BF16 = jnp.bfloat16

RMS_EPS = 1e-6
LN_EPS = 1e-5
LAMBDA_INIT = 0.8 - 0.6 * math.exp(-0.3 * 0)
NEG_BIG = -1e30

V7X_VMEM_BYTES = 64 * 1024 * 1024
VMEM_LIMIT = V7X_VMEM_BYTES - 8 * 1024 * 1024
SUBLANES = 8
LANES = 128

ROW_TILE = 512
FF_TILE = 512
MIXC_TILE = 512
CONV_TIME_TILE = 128
CONV_HALO = 32
ATTN_TILE = 512
PAGES_PER_STEP = 4


def _row_tile(m, target):
    t = min(m, target)
    while m % t or t % SUBLANES:
        t -= 1
    return t


def _params(*sem):
    return pltpu.CompilerParams(dimension_semantics=sem, vmem_limit_bytes=VMEM_LIMIT)


def _rmsnorm(x, g):
    return x * lax.rsqrt(jnp.mean(x * x, axis=-1, keepdims=True) + RMS_EPS) * g


def _silu(x):
    return x * jax.nn.sigmoid(x)


def _ffn_kernel(x_ref, gn_ref, wg_ref, wu_ref, wd_ref, fn_ref, o_ref, xn_ref, *, final_norm):
    f = pl.program_id(1)

    @pl.when(f == 0)
    def _():
        xn_ref[...] = _rmsnorm(x_ref[...], gn_ref[...]).astype(BF16)

    xn = xn_ref[...]
    g = jnp.dot(xn, wg_ref[...], preferred_element_type=F32)
    u = jnp.dot(xn, wu_ref[...], preferred_element_type=F32)
    h = (_silu(g) * u).astype(BF16)
    d = jnp.dot(h, wd_ref[...], preferred_element_type=F32)

    @pl.when(f == 0)
    def _():
        o_ref[...] = d

    @pl.when(f > 0)
    def _():
        o_ref[...] += d

    @pl.when(f == pl.num_programs(1) - 1)
    def _():
        y = x_ref[...] + 0.5 * o_ref[...]
        if final_norm:
            y = _rmsnorm(y, fn_ref[...])
        o_ref[...] = y


def _ffn(x, gn, wg, wu, wd, fn, *, final_norm):
    m, d = x.shape
    ff = wg.shape[1]
    tm = _row_tile(m, ROW_TILE)
    tf = _row_tile(ff, FF_TILE)
    return pl.pallas_call(
        functools.partial(_ffn_kernel, final_norm=final_norm),
        grid=(m // tm, ff // tf),
        in_specs=[
            pl.BlockSpec((tm, d), lambda i, f: (i, 0)),
            pl.BlockSpec((1, d), lambda i, f: (0, 0)),
            pl.BlockSpec((d, tf), lambda i, f: (0, f)),
            pl.BlockSpec((d, tf), lambda i, f: (0, f)),
            pl.BlockSpec((tf, d), lambda i, f: (f, 0)),
            pl.BlockSpec((1, d), lambda i, f: (0, 0)),
        ],
        out_specs=pl.BlockSpec((tm, d), lambda i, f: (i, 0)),
        out_shape=jax.ShapeDtypeStruct((m, d), F32),
        scratch_shapes=[pltpu.VMEM((tm, d), BF16)],
        compiler_params=_params("parallel", "arbitrary"),
        name="ffn",
    )(x, gn.reshape(1, d), wg, wu, wd, fn.reshape(1, d))


def _proj_ab_kernel(x_ref, gn_ref, w_ref, g_ref, q_ref, k_ref, kb_ref, v_ref, vb_ref,
                    xn_ref, val_ref, *, q_scale):
    j = pl.program_id(1)

    @pl.when(j == 0)
    def _():
        xn_ref[...] = _rmsnorm(x_ref[...], gn_ref[...]).astype(BF16)

    z = jnp.dot(xn_ref[...], w_ref[...], preferred_element_type=F32)

    @pl.when(j == 0)
    def _():
        val_ref[...] = z

    @pl.when(j == 1)
    def _():
        g_ref[...] = val_ref[...] * jax.nn.sigmoid(z)

    @pl.when(j == 2)
    def _():
        q_ref[...] = (z * q_scale).astype(BF16)

    @pl.when(j == 3)
    def _():
        k_ref[...] = z
        kb_ref[...] = z.astype(BF16)

    @pl.when(j == 4)
    def _():
        v_ref[...] = z
        vb_ref[...] = z.astype(BF16)


def _proj_ab(x, gn, w, seg, q_scale):
    m, d = x.shape
    assert w.shape[1] == 5 * seg
    tm = _row_tile(m, ROW_TILE)
    row = lambda i, j: (i, 0)
    out = lambda dt: jax.ShapeDtypeStruct((m, seg), dt)
    return pl.pallas_call(
        functools.partial(_proj_ab_kernel, q_scale=q_scale),
        grid=(m // tm, 5),
        in_specs=[
            pl.BlockSpec((tm, d), row),
            pl.BlockSpec((1, d), lambda i, j: (0, 0)),
            pl.BlockSpec((d, seg), lambda i, j: (0, j)),
        ],
        out_specs=[pl.BlockSpec((tm, seg), row)] * 6,
        out_shape=[out(F32), out(BF16), out(F32), out(BF16), out(F32), out(BF16)],
        scratch_shapes=[pltpu.VMEM((tm, d), BF16), pltpu.VMEM((tm, seg), F32)],
        compiler_params=_params("parallel", "arbitrary"),
        name="proj_ab",
    )(x, gn.reshape(1, d), w)


def _ln_silu(a, lg, lb):
    mu = jnp.mean(a, axis=-1, keepdims=True)
    c = a - mu
    var = jnp.mean(c * c, axis=-1, keepdims=True)
    return _silu(c * lax.rsqrt(var + LN_EPS) * lg + lb)


def _conv_a_kernel(g_ref, pre_ref, w_ref, b_ref, lg_ref, lb_ref, a_ref, buf_ref, c_ref, *, width):
    t = pl.program_id(1)
    tt, ch = g_ref.shape

    @pl.when(t == 0)
    def _():
        buf_ref[0:CONV_HALO, :] = pre_ref[0]

    @pl.when(t > 0)
    def _():
        buf_ref[0:CONV_HALO, :] = buf_ref[tt:tt + CONV_HALO, :]

    buf_ref[CONV_HALO:CONV_HALO + tt, :] = g_ref[...]
    off = CONV_HALO - (width - 1)
    cc, rr = 2 * LANES, 4 * SUBLANES
    for c0 in range(0, ch, cc):
        for r0 in range(0, tt, rr):
            acc = jnp.zeros((rr, cc), F32)
            for j in range(width):
                acc = acc + w_ref[j:j + 1, c0:c0 + cc] * buf_ref[r0 + off + j:r0 + off + j + rr, c0:c0 + cc]
            c_ref[r0:r0 + rr, c0:c0 + cc] = acc + b_ref[:, c0:c0 + cc]
    ln_rows = 8 * SUBLANES
    for r0 in range(0, tt, ln_rows):
        y = _ln_silu(c_ref[r0:r0 + ln_rows, :], lg_ref[...], lb_ref[...])
        a_ref[r0:r0 + ln_rows, :] = y.astype(BF16)


def _conv_a(g, prefix, w, b, lg, lb, batch):
    m, ch = g.shape
    seq = m // batch
    width = w.shape[0]
    assert width - 1 <= CONV_HALO
    tt = _row_tile(seq, CONV_TIME_TILE)
    assert tt >= CONV_HALO and tt % (8 * SUBLANES) == 0
    nt = seq // tt
    pre = jnp.pad(prefix, ((0, 0), (CONV_HALO - (width - 1), 0), (0, 0)))
    vec = lambda: pl.BlockSpec((1, ch), lambda bi, t: (0, 0))
    return pl.pallas_call(
        functools.partial(_conv_a_kernel, width=width),
        grid=(batch, nt),
        in_specs=[
            pl.BlockSpec((tt, ch), lambda bi, t: (bi * nt + t, 0)),
            pl.BlockSpec((1, CONV_HALO, ch), lambda bi, t: (bi, 0, 0)),
            pl.BlockSpec((width, ch), lambda bi, t: (0, 0)),
            vec(), vec(), vec(),
        ],
        out_specs=pl.BlockSpec((tt, ch), lambda bi, t: (bi * nt + t, 0)),
        out_shape=jax.ShapeDtypeStruct((m, ch), BF16),
        scratch_shapes=[pltpu.VMEM((CONV_HALO + tt, ch), F32), pltpu.VMEM((tt, ch), F32)],
        compiler_params=_params("parallel", "arbitrary"),
        name="conv_a",
    )(g, pre, w, b.reshape(1, ch), lg.reshape(1, ch), lb.reshape(1, ch))


def _conv_a_short_kernel(h_ref, w_ref, b_ref, lg_ref, lb_ref, a_ref, *, width):
    steps = a_ref.shape[0]
    for t in range(steps):
        acc = jnp.zeros(a_ref.shape[1:], F32)
        for j in range(width):
            acc = acc + w_ref[j:j + 1, :] * h_ref[t + j]
        a_ref[t] = _ln_silu(acc + b_ref[...], lg_ref[...], lb_ref[...]).astype(BF16)


def _conv_a_short(hist, w, b, lg, lb):
    n, batch, ch = hist.shape
    width = w.shape[0]
    steps = n - (width - 1)
    return pl.pallas_call(
        functools.partial(_conv_a_short_kernel, width=width),
        out_shape=jax.ShapeDtypeStruct((steps, batch, ch), BF16),
        compiler_params=pltpu.CompilerParams(vmem_limit_bytes=VMEM_LIMIT),
        name="conv_a_short",
    )(hist, w, b.reshape(1, ch), lg.reshape(1, ch), lb.reshape(1, ch))


def _lambda(lp):
    e1 = jnp.exp(jnp.sum(lp[0:1] * lp[1:2], axis=-1, keepdims=True))
    e2 = jnp.exp(jnp.sum(lp[2:3] * lp[3:4], axis=-1, keepdims=True))
    return e1 - e2 + LAMBDA_INIT


def _softmax_update(s, v, idx, m_ref, l_ref, acc_ref):
    m_prev = m_ref[idx]
    m_new = jnp.maximum(m_prev, jnp.max(s, axis=-1, keepdims=True))
    alpha = jnp.exp(m_prev - m_new)
    p = jnp.exp(s - m_new[:, :1])
    l_ref[idx] = alpha * l_ref[idx] + jnp.sum(p, axis=-1, keepdims=True)
    acc_ref[idx] = alpha * acc_ref[idx] + jnp.dot(p.astype(BF16), v, preferred_element_type=F32)
    m_ref[idx] = m_new


def _head_norm(o, g):
    return o * lax.rsqrt(jnp.mean(o * o, axis=-1, keepdims=True) + RMS_EPS) * g * (1.0 - LAMBDA_INIT)


_NT = (((1,), (1,)), ((), ()))


def _attn_prompt_kernel(lam_ref, g_ref, q_ref, k_ref, v_ref, o_ref, m_ref, l_ref, acc_ref, *, n_heads):
    qi = pl.program_id(1)
    ki = pl.program_id(2)
    tq = q_ref.shape[0]
    tk = k_ref.shape[0]
    hd = q_ref.shape[1] // n_heads

    @pl.when(ki == 0)
    def _():
        m_ref[...] = jnp.full(m_ref.shape, NEG_BIG, F32)
        l_ref[...] = jnp.zeros(l_ref.shape, F32)
        acc_ref[...] = jnp.zeros(acc_ref.shape, F32)

    def step(diagonal):
        rel = (lax.broadcasted_iota(jnp.int32, (tq, tk), 1) - lax.broadcasted_iota(jnp.int32, (tq, tk), 0)
               + (ki * tk - qi * tq)).astype(F32)
        lane = lax.broadcasted_iota(jnp.int32, (tq, hd), 1)
        for h in range(n_heads):
            cols = slice(h * hd, (h + 1) * hd)
            qh = q_ref[:, cols]
            kh = k_ref[:, cols]
            vh = v_ref[:, cols]
            bias = (2.0 ** -(8.0 * (h + 1) / n_heads)) * rel
            if diagonal:
                bias = jnp.where(rel <= 0, bias, NEG_BIG)
            for mp in range(2):
                half = (lane < hd // 2) if mp == 0 else (lane >= hd // 2)
                qm = jnp.where(half, qh, jnp.zeros_like(qh))
                s = lax.dot_general(qm, kh, _NT, preferred_element_type=F32) + bias
                _softmax_update(s, vh, 2 * h + mp, m_ref, l_ref, acc_ref)

    @pl.when(ki < qi)
    def _():
        step(False)

    @pl.when(ki == qi)
    def _():
        step(True)
        lam = _lambda(lam_ref[...])
        for h in range(n_heads):
            o = acc_ref[2 * h] / l_ref[2 * h] - lam * (acc_ref[2 * h + 1] / l_ref[2 * h + 1])
            o_ref[:, h * hd:(h + 1) * hd] = _head_norm(o, g_ref[...]).astype(BF16)


def _attn_prompt(q, k, v, lam_params, subln_g, batch, n_heads):
    m, ad = q.shape
    seq = m // batch
    hd = ad // n_heads
    tq = _row_tile(seq, ATTN_TILE)
    nq = seq // tq
    kv_spec = pl.BlockSpec((tq, ad), lambda b, qi, ki: (b * nq + jnp.minimum(ki, qi), 0))
    return pl.pallas_call(
        functools.partial(_attn_prompt_kernel, n_heads=n_heads),
        grid=(batch, nq, nq),
        in_specs=[
            pl.BlockSpec(lam_params.shape, lambda b, qi, ki: (0, 0)),
            pl.BlockSpec((1, hd), lambda b, qi, ki: (0, 0)),
            pl.BlockSpec((tq, ad), lambda b, qi, ki: (b * nq + qi, 0)),
            kv_spec, kv_spec,
        ],
        out_specs=pl.BlockSpec((tq, ad), lambda b, qi, ki: (b * nq + qi, 0)),
        out_shape=jax.ShapeDtypeStruct((m, ad), BF16),
        scratch_shapes=[pltpu.VMEM((2 * n_heads, tq, hd), F32)] * 3,
        compiler_params=_params("parallel", "parallel", "arbitrary"),
        name="attn_prompt",
    )(lam_params, subln_g.reshape(1, hd), q, k, v)


def _attn_sample_kernel(pt_ref, lam_ref, g_ref, q_ref, kn_ref, vn_ref, *rest, pp, past, n_heads):
    k_pages = rest[:pp]
    v_pages = rest[pp:2 * pp]
    o_ref, m_ref, l_ref, acc_ref = rest[2 * pp:]
    st = pl.program_id(1)
    r = q_ref.shape[1]
    page_tokens = k_pages[0].shape[1]
    pc = page_tokens * n_heads
    hshift = n_heads.bit_length() - 1
    assert 1 << hshift == n_heads

    @pl.when(st == 0)
    def _():
        m_ref[...] = jnp.full(m_ref.shape, NEG_BIG, F32)
        l_ref[...] = jnp.zeros(l_ref.shape, F32)
        acc_ref[...] = jnp.zeros(acc_ref.shape, F32)

    q = q_ref[0]
    lane = lax.broadcasted_iota(jnp.int32, q.shape, 1)
    zero = jnp.zeros_like(q)
    half = q.shape[1] // 2
    qm = jnp.concatenate([jnp.where(lane < half, q, zero), jnp.where(lane >= half, q, zero)], axis=0)
    row = lax.broadcasted_iota(jnp.int32, (2 * r, 1), 0)
    row_head = row & (n_heads - 1)
    row_step = (row & (r - 1)) >> hshift
    slope = jnp.zeros((2 * r, 1), F32)
    for h in range(n_heads):
        slope = jnp.where(row_head == h, 2.0 ** -(8.0 * (h + 1) / n_heads), slope)

    def update(s, v_list):
        m_prev = m_ref[...]
        m_new = jnp.maximum(m_prev, jnp.max(s, axis=-1, keepdims=True))
        alpha = jnp.exp(m_prev - m_new)
        p = jnp.exp(s - m_new[:, :1])
        l_ref[...] = alpha * l_ref[...] + jnp.sum(p, axis=-1, keepdims=True)
        acc = alpha * acc_ref[...]
        c0 = 0
        for v in v_list:
            acc = acc + jnp.dot(p[:, c0:c0 + v.shape[0]].astype(BF16), v, preferred_element_type=F32)
            c0 += v.shape[0]
        acc_ref[...] = acc
        m_ref[...] = m_new

    col = lax.broadcasted_iota(jnp.int32, (1, pc), 1)
    valid = (col & (n_heads - 1)) == row_head
    rel0 = ((col >> hshift) - row_step - past).astype(F32)
    scores, values = [], []
    for p in range(pp):
        kp = k_pages[p][0].reshape(pc, q.shape[1]).astype(BF16)
        base = ((st * pp + p) * page_tokens).astype(F32)
        s = lax.dot_general(qm, kp, _NT, preferred_element_type=F32) + slope * (rel0 + base)
        scores.append(jnp.where(valid, s, NEG_BIG))
        values.append(v_pages[p][0].reshape(pc, q.shape[1]).astype(BF16))
    update(jnp.concatenate(scores, axis=1), values)

    @pl.when(st == pl.num_programs(1) - 1)
    def _():
        ncol = lax.broadcasted_iota(jnp.int32, (1, r), 1)
        dist = ((ncol >> hshift) - row_step)
        ok = ((ncol & (n_heads - 1)) == row_head) & (dist <= 0)
        s = lax.dot_general(qm, kn_ref[0], _NT, preferred_element_type=F32) + slope * dist.astype(F32)
        update(jnp.where(ok, s, NEG_BIG), [vn_ref[0]])
        lam = _lambda(lam_ref[...])
        o = acc_ref[0:r] / l_ref[0:r] - lam * (acc_ref[r:2 * r] / l_ref[r:2 * r])
        o_ref[0] = _head_norm(o, g_ref[...]).astype(BF16)


def _attn_sample(q, kn, vn, cache_k, cache_v, page_table, lam_params, subln_g, n_heads):
    batch, r, hd = q.shape
    n_pages = page_table.shape[1]
    page_tokens = cache_k.shape[1]
    pp = PAGES_PER_STEP
    while n_pages % pp:
        pp -= 1
    assert r & (r - 1) == 0
    page_specs = [
        pl.BlockSpec((1,) + cache_k.shape[1:], functools.partial(lambda b, s, pt, p: (pt[b, s * pp + p], 0, 0, 0), p=p))
        for p in range(pp)
    ]
    new_spec = pl.BlockSpec((1, r, hd), lambda b, s, pt: (b, 0, 0))
    grid_spec = pltpu.PrefetchScalarGridSpec(
        num_scalar_prefetch=1,
        grid=(batch, n_pages // pp),
        in_specs=[
            pl.BlockSpec(lam_params.shape, lambda b, s, pt: (0, 0)),
            pl.BlockSpec((1, hd), lambda b, s, pt: (0, 0)),
            new_spec, new_spec, new_spec,
        ] + page_specs + page_specs,
        out_specs=new_spec,
        scratch_shapes=[pltpu.VMEM((2 * r, hd), F32)] * 3,
    )
    return pl.pallas_call(
        functools.partial(_attn_sample_kernel, pp=pp, past=n_pages * page_tokens, n_heads=n_heads),
        grid_spec=grid_spec,
        out_shape=jax.ShapeDtypeStruct((batch, r, hd), BF16),
        compiler_params=_params("parallel", "arbitrary"),
        name="attn_sample",
    )(page_table, lam_params, subln_g.reshape(1, hd), q, kn, vn,
      *([cache_k] * pp), *([cache_v] * pp))


def _out_ab_kernel(x_ref, a_ref, att_ref, w_ref, o_ref):
    ca = a_ref.shape[1]
    o_ref[...] = (x_ref[...]
                  + jnp.dot(a_ref[...], w_ref[0:ca, :], preferred_element_type=F32)
                  + jnp.dot(att_ref[...], w_ref[ca:, :], preferred_element_type=F32))


def _out_ab(x, a, att, w):
    m, d = x.shape
    tm = _row_tile(m, ROW_TILE)
    return pl.pallas_call(
        _out_ab_kernel,
        grid=(m // tm,),
        in_specs=[
            pl.BlockSpec((tm, d), lambda i: (i, 0)),
            pl.BlockSpec((tm, a.shape[1]), lambda i: (i, 0)),
            pl.BlockSpec((tm, att.shape[1]), lambda i: (i, 0)),
            pl.BlockSpec(w.shape, lambda i: (0, 0)),
        ],
        out_specs=pl.BlockSpec((tm, d), lambda i: (i, 0)),
        out_shape=jax.ShapeDtypeStruct((m, d), F32),
        compiler_params=_params("parallel"),
        name="out_ab",
    )(x, a, att, w)


def _mixer_c_kernel(x_ref, gn_ref, wb_ref, wc_ref, wh_ref, cw_ref, wo_ref, *rest, width, tiles_per_seq, rows_per_seq):
    short = rows_per_seq is not None
    if short:
        hist_ref, o_ref, u_ref, xn_ref, ubuf_ref = rest
    else:
        o_ref, tail_ref, xn_ref, ubuf_ref, carry_ref = rest
    i = pl.program_id(0)
    c = pl.program_id(1)
    tm = x_ref.shape[0]
    tc = wb_ref.shape[1]

    @pl.when(c == 0)
    def _():
        xn_ref[...] = _rmsnorm(x_ref[...], gn_ref[...]).astype(BF16)

    xn = xn_ref[...]
    gate_b = jnp.dot(xn, wb_ref[...], preferred_element_type=F32)
    u = jnp.dot(xn, wc_ref[...], preferred_element_type=F32) * jnp.dot(xn, wh_ref[...], preferred_element_type=F32)
    ubuf_ref[SUBLANES:SUBLANES + tm, :] = u
    if short:
        u_ref[...] = u
        ubuf_ref[0:SUBLANES, :] = jnp.zeros((SUBLANES, tc), F32)
    else:
        tail_ref[0] = u[tm - SUBLANES:tm, :]

        @pl.when(i % tiles_per_seq == 0)
        def _():
            ubuf_ref[0:SUBLANES, :] = jnp.zeros((SUBLANES, tc), F32)

        @pl.when(i % tiles_per_seq != 0)
        def _():
            ubuf_ref[0:SUBLANES, :] = carry_ref[c]

        carry_ref[c] = u[tm - SUBLANES:tm, :]
    acc = cw_ref[width - 1:width, :] * u
    for j in range(1, width):
        prev = ubuf_ref[SUBLANES - j:SUBLANES - j + tm, :]
        if short:
            pos = lax.broadcasted_iota(jnp.int32, (tm, 1), 0) & (rows_per_seq - 1)
            prev = jnp.where(pos >= j, prev, hist_ref[j - 1])
        acc = acc + cw_ref[width - 1 - j:width - j, :] * prev
    y = (gate_b * acc).astype(BF16)
    d = jnp.dot(y, wo_ref[...], preferred_element_type=F32)

    @pl.when(c == 0)
    def _():
        o_ref[...] = x_ref[...] + d

    @pl.when(c > 0)
    def _():
        o_ref[...] += d


def _mixer_c(x, gn, w_in, conv_w, w_out, batch, prefix):
    m, d = x.shape
    ch = w_out.shape[0]
    width = conv_w.shape[0]
    assert w_in.shape[1] == 3 * ch and width - 1 <= SUBLANES
    seq = m // batch
    tm = _row_tile(m, ROW_TILE)
    tc = _row_tile(ch, MIXC_TILE)
    nc = ch // tc
    short = seq < tm
    in_specs = [
        pl.BlockSpec((tm, d), lambda i, c: (i, 0)),
        pl.BlockSpec((1, d), lambda i, c: (0, 0)),
        pl.BlockSpec((d, tc), lambda i, c: (0, c)),
        pl.BlockSpec((d, tc), lambda i, c: (0, nc + c)),
        pl.BlockSpec((d, tc), lambda i, c: (0, 2 * nc + c)),
        pl.BlockSpec((width, tc), lambda i, c: (0, c)),
        pl.BlockSpec((tc, d), lambda i, c: (c, 0)),
    ]
    args = [x, gn.reshape(1, d), w_in, w_in, w_in, conv_w, w_out]
    o_spec = pl.BlockSpec((tm, d), lambda i, c: (i, 0))
    o_shape = jax.ShapeDtypeStruct((m, d), F32)
    scratch = [pltpu.VMEM((tm, d), BF16), pltpu.VMEM((SUBLANES + tm, tc), F32)]
    if short:
        assert tm % seq == 0 and seq & (seq - 1) == 0
        padded = jnp.concatenate([prefix, jnp.zeros((batch, seq, ch), F32)], axis=1)
        hist = jnp.stack([padded[:, width - 1 - j:width - 1 - j + seq].reshape(m, ch) for j in range(1, width)])
        in_specs.append(pl.BlockSpec((width - 1, tm, tc), lambda i, c: (0, i, c)))
        args.append(hist)
        out_specs = [o_spec, pl.BlockSpec((tm, tc), lambda i, c: (i, c))]
        out_shape = [o_shape, jax.ShapeDtypeStruct((m, ch), F32)]
        kern = functools.partial(_mixer_c_kernel, width=width, tiles_per_seq=None, rows_per_seq=seq)
    else:
        assert seq % tm == 0 and prefix is None
        out_specs = [o_spec, pl.BlockSpec((1, SUBLANES, tc), lambda i, c: (i, 0, c))]
        out_shape = [o_shape, jax.ShapeDtypeStruct((m // tm, SUBLANES, ch), F32)]
        scratch.append(pltpu.VMEM((nc, SUBLANES, tc), F32))
        kern = functools.partial(_mixer_c_kernel, width=width, tiles_per_seq=seq // tm, rows_per_seq=None)
    o, aux = pl.pallas_call(
        kern,
        grid=(m // tm, nc),
        in_specs=in_specs,
        out_specs=out_specs,
        out_shape=out_shape,
        scratch_shapes=scratch,
        compiler_params=_params("arbitrary", "arbitrary"),
        name="mixer_c",
    )(*args)
    if short:
        u_hist = jnp.concatenate([prefix, aux.reshape(batch, seq, ch)], axis=1)
        state = u_hist[:, -(width - 1):]
    else:
        tails = aux.reshape(batch, seq // tm, SUBLANES, ch)
        state = tails[:, -1, SUBLANES - (width - 1):, :]
    return o, state


def _forward(x3, w, conv_a_prefix, conv_c_prefix, attend):
    batch, seq, d = x3.shape
    m = batch * seq
    x = x3.reshape(m, d)
    seg = w["conv_a_w"].shape[1]
    hd = w["diff_subln_g"].shape[0]
    n_heads = seg // hd
    width_a = w["conv_a_w"].shape[0]

    x = _ffn(x, w["norm_ffn1"][0], w["ffn1_w_gate"][0], w["ffn1_w_up"][0], w["ffn1_w_down"][0],
             w["final_norm"], final_norm=False)
    g, q, k, kb, v, vb = _proj_ab(x, w["norm_mix"][0], w["w_in_ab"], seg, (hd // 2) ** -0.5)
    conv_args = (w["conv_a_w"], w["conv_a_b"], w["conv_a_ln_g"], w["conv_a_ln_b"])
    g3 = g.reshape(batch, seq, seg)
    if seq >= CONV_HALO:
        prefix = jnp.zeros((batch, width_a - 1, seg), F32) if conv_a_prefix is None else conv_a_prefix
        a = _conv_a(g, prefix, *conv_args, batch)
        g_hist = g3 if conv_a_prefix is None else jnp.concatenate([conv_a_prefix, g3], axis=1)
    else:
        g_hist = jnp.concatenate([conv_a_prefix, g3], axis=1)
        a = _conv_a_short(jnp.swapaxes(g_hist, 0, 1), *conv_args)
        a = jnp.swapaxes(a, 0, 1).reshape(m, seg)
    conv_a_state = g_hist[:, -(width_a - 1):]
    att = attend(q, kb, vb, batch, seq, n_heads, hd)
    x = _out_ab(x, a, att, w["w_out_ab"])
    x = _ffn(x, w["norm_ffn2"][0], w["ffn2_w_gate"][0], w["ffn2_w_up"][0], w["ffn2_w_down"][0],
             w["final_norm"], final_norm=False)

    x = _ffn(x, w["norm_ffn1"][1], w["ffn1_w_gate"][1], w["ffn1_w_up"][1], w["ffn1_w_down"][1],
             w["final_norm"], final_norm=False)
    x, conv_c_state = _mixer_c(x, w["norm_mix"][1], w["w_in_c"], w["conv_c_w"], w["w_out_c"], batch, conv_c_prefix)
    y = _ffn(x, w["norm_ffn2"][1], w["ffn2_w_gate"][1], w["ffn2_w_up"][1], w["ffn2_w_down"][1],
             w["final_norm"], final_norm=True)
    return (y.reshape(batch, seq, d), k.reshape(batch, seq, n_heads, hd), v.reshape(batch, seq, n_heads, hd),
            conv_a_state, conv_c_state)


def kernel(x_prompt, x_sample, cache_k, cache_v, state_conv_a, state_conv_c, page_table, norm_ffn1, ffn1_w_gate, ffn1_w_up, ffn1_w_down, norm_mix, norm_ffn2, ffn2_w_gate, ffn2_w_up, ffn2_w_down, w_in_ab, conv_a_w, conv_a_b, conv_a_ln_g, conv_a_ln_b, diff_lambda, diff_subln_g, w_out_ab, w_in_c, conv_c_w, w_out_c, final_norm):
    w = dict(
        norm_ffn1=norm_ffn1, norm_mix=norm_mix, norm_ffn2=norm_ffn2, final_norm=final_norm,
        ffn1_w_gate=ffn1_w_gate.astype(BF16), ffn1_w_up=ffn1_w_up.astype(BF16), ffn1_w_down=ffn1_w_down.astype(BF16),
        ffn2_w_gate=ffn2_w_gate.astype(BF16), ffn2_w_up=ffn2_w_up.astype(BF16), ffn2_w_down=ffn2_w_down.astype(BF16),
        w_in_ab=w_in_ab.astype(BF16), w_out_ab=w_out_ab.astype(BF16),
        w_in_c=w_in_c.astype(BF16), w_out_c=w_out_c.astype(BF16),
        conv_a_w=conv_a_w, conv_a_b=conv_a_b, conv_a_ln_g=conv_a_ln_g, conv_a_ln_b=conv_a_ln_b,
        conv_c_w=conv_c_w, diff_subln_g=diff_subln_g,
    )

    def attend_prompt(q, kb, vb, batch, seq, n_heads, hd):
        return _attn_prompt(q, kb, vb, diff_lambda, diff_subln_g, batch, n_heads)

    def attend_sample(q, kb, vb, batch, seq, n_heads, hd):
        rows = lambda t: t.reshape(batch, seq * n_heads, hd)
        att = _attn_sample(rows(q), rows(kb), rows(vb), cache_k, cache_v, page_table,
                           diff_lambda, diff_subln_g, n_heads)
        return att.reshape(batch * seq, n_heads * hd)

    y_p, k_p, v_p, ca_p, cc_p = _forward(x_prompt, w, None, None, attend_prompt)
    y_s, k_s, v_s, ca_s, cc_s = _forward(x_sample, w, state_conv_a, state_conv_c, attend_sample)
    return (y_p, y_s, k_p, v_p, ca_p, cc_p, k_s, v_s, ca_s, cc_s)
```

```python
import functools
import math

import jax
import jax.numpy as jnp
from jax import lax
from jax.experimental import pallas as pl
from jax.experimental.pallas import tpu as pltpu

F32 = jnp.float32
BF16 = jnp.bfloat16

RMS_EPS = 1e-6
LN_EPS = 1e-5
LAMBDA_INIT = 0.8 - 0.6 * math.exp(-0.3 * 0)
NEG_BIG = -1e30

V7X_VMEM_BYTES = 64 * 1024 * 1024
VMEM_LIMIT = V7X_VMEM_BYTES - 8 * 1024 * 1024
SUBLANES = 8
LANES = 128

ROW_TILE = 512
FF_TILE = 512
MIXC_TILE = 512
CONV_TIME_TILE = 256
CONV_HALO = 32
ATTN_TILE = 512
PAGES_PER_STEP = 8


def _row_tile(m, target):
    t = min(m, target)
    while m % t or t % SUBLANES:
        t -= 1
    return t


def _params(*sem):
    return pltpu.CompilerParams(dimension_semantics=sem, vmem_limit_bytes=VMEM_LIMIT)


def _rmsnorm(x, g):
    return x * lax.rsqrt(jnp.mean(x * x, axis=-1, keepdims=True) + RMS_EPS) * g


def _silu(x):
    return x * jax.nn.sigmoid(x)


def _ffn_kernel(x_ref, gn_ref, wg_ref, wu_ref, wd_ref, fn_ref, o_ref, xn_ref, *, final_norm):
    f = pl.program_id(1)

    @pl.when(f == 0)
    def _():
        xn_ref[...] = _rmsnorm(x_ref[...], gn_ref[...]).astype(BF16)
        o_ref[...] = jnp.zeros(o_ref.shape, F32)

    xn = xn_ref[...]
    g = jnp.dot(xn, wg_ref[...], preferred_element_type=F32)
    u = jnp.dot(xn, wu_ref[...], preferred_element_type=F32)
    h = (_silu(g) * u).astype(BF16)
    o_ref[...] += jnp.dot(h, wd_ref[...], preferred_element_type=F32)

    @pl.when(f == pl.num_programs(1) - 1)
    def _():
        y = x_ref[...] + 0.5 * o_ref[...]
        if final_norm:
            y = _rmsnorm(y, fn_ref[...])
        o_ref[...] = y


def _ffn(x, gn, wg, wu, wd, fn, layer, *, final_norm):
    m, d = x.shape
    ff = wg.shape[2]
    tm = _row_tile(m, ROW_TILE)
    tf = _row_tile(ff, FF_TILE)
    return pl.pallas_call(
        functools.partial(_ffn_kernel, final_norm=final_norm),
        grid=(m // tm, ff // tf),
        in_specs=[
            pl.BlockSpec((tm, d), lambda i, f: (i, 0)),
            pl.BlockSpec((None, 1, d), lambda i, f: (layer, 0, 0)),
            pl.BlockSpec((None, d, tf), lambda i, f: (layer, 0, f)),
            pl.BlockSpec((None, d, tf), lambda i, f: (layer, 0, f)),
            pl.BlockSpec((None, tf, d), lambda i, f: (layer, f, 0)),
            pl.BlockSpec((1, d), lambda i, f: (0, 0)),
        ],
        out_specs=pl.BlockSpec((tm, d), lambda i, f: (i, 0)),
        out_shape=jax.ShapeDtypeStruct((m, d), F32),
        scratch_shapes=[pltpu.VMEM((tm, d), BF16)],
        compiler_params=_params("parallel", "arbitrary"),
        name="ffn",
    )(x, gn.reshape(gn.shape[0], 1, d), wg, wu, wd, fn.reshape(1, d))


def _proj_ab_kernel(x_ref, gn_ref, w_ref, zf_ref, zb_ref, xn_ref, *, q_scale):
    j = pl.program_id(1)

    @pl.when(j == 0)
    def _():
        xn_ref[...] = _rmsnorm(x_ref[...], gn_ref[...]).astype(BF16)

    z = jnp.dot(xn_ref[...], w_ref[...], preferred_element_type=F32)
    zf_ref[...] = z
    zb_ref[...] = (z * jnp.where(j == 0, q_scale, 1.0)).astype(BF16)


def _proj_ab(x, gn, w, seg, q_scale):
    m, d = x.shape
    assert w.shape[1] == 5 * seg
    tm = _row_tile(m, ROW_TILE)
    w_col = lambda j: jnp.where(j == 0, 2, jnp.where(j <= 2, j - 1, j))
    zb_blk = lambda j: jnp.where(j == 0, 0, jnp.where(j <= 3, 1, 2))
    return pl.pallas_call(
        functools.partial(_proj_ab_kernel, q_scale=q_scale),
        grid=(m // tm, 5),
        in_specs=[
            pl.BlockSpec((tm, d), lambda i, j: (i, 0)),
            pl.BlockSpec((1, d), lambda i, j: (0, 0)),
            pl.BlockSpec((d, seg), lambda i, j: (0, w_col(j))),
        ],
        out_specs=[
            pl.BlockSpec((None, tm, seg), lambda i, j: (jnp.maximum(j - 1, 0), i, 0)),
            pl.BlockSpec((None, tm, seg), lambda i, j: (zb_blk(j), i, 0)),
        ],
        out_shape=[jax.ShapeDtypeStruct((4, m, seg), F32), jax.ShapeDtypeStruct((3, m, seg), BF16)],
        scratch_shapes=[pltpu.VMEM((tm, d), BF16)],
        compiler_params=_params("parallel", "arbitrary"),
        name="proj_ab",
    )(x, gn.reshape(1, d), w)


def _ln_silu(a, lg, lb):
    mu = jnp.mean(a, axis=-1, keepdims=True)
    c = a - mu
    var = jnp.mean(c * c, axis=-1, keepdims=True)
    return _silu(c * lax.rsqrt(var + LN_EPS) * lg + lb)


def _conv_a_kernel(val_ref, gate_ref, pre_ref, w_ref, b_ref, lg_ref, lb_ref, a_ref, tail_ref,
                   buf_ref, sh_ref, c_ref, *, width):
    t = pl.program_id(1)
    tt, ch = val_ref.shape

    @pl.when(t == 0)
    def _():
        buf_ref[0:CONV_HALO, :] = pre_ref[0]

    @pl.when(t > 0)
    def _():
        buf_ref[0:CONV_HALO, :] = buf_ref[tt:tt + CONV_HALO, :]

    buf_ref[CONV_HALO:CONV_HALO + tt, :] = val_ref[...] * jax.nn.sigmoid(gate_ref[...])
    tail_ref[0] = buf_ref[tt:tt + CONV_HALO, :]
    n_sh = sh_ref.shape[1]
    for s in range(1, SUBLANES):
        sh_ref[s - 1] = buf_ref[s:s + n_sh, :]
    off = CONV_HALO - (width - 1)
    cc, rr = 2 * LANES, 8 * SUBLANES
    for c0 in range(0, ch, cc):
        for r0 in range(0, tt, rr):
            acc = jnp.zeros((rr, cc), F32)
            for j in range(width):
                s = (off + j) % SUBLANES
                a0 = off + j - s + r0
                if s == 0:
                    win = buf_ref[a0:a0 + rr, c0:c0 + cc]
                else:
                    win = sh_ref[s - 1, a0:a0 + rr, c0:c0 + cc]
                acc = acc + w_ref[j:j + 1, c0:c0 + cc] * win
            c_ref[r0:r0 + rr, c0:c0 + cc] = acc + b_ref[:, c0:c0 + cc]
    ln_rows = 8 * SUBLANES
    for r0 in range(0, tt, ln_rows):
        y = _ln_silu(c_ref[r0:r0 + ln_rows, :], lg_ref[...], lb_ref[...])
        a_ref[r0:r0 + ln_rows, :] = y.astype(BF16)


def _conv_a(zf, prefix, w, b, lg, lb, batch):
    _, m, ch = zf.shape
    seq = m // batch
    width = w.shape[0]
    assert width - 1 <= CONV_HALO
    tt = _row_tile(seq, CONV_TIME_TILE)
    assert tt >= CONV_HALO and tt % (8 * SUBLANES) == 0
    nt = seq // tt
    pre = jnp.pad(prefix, ((0, 0), (CONV_HALO - (width - 1), 0), (0, 0)))
    vec = lambda: pl.BlockSpec((1, ch), lambda bi, t: (0, 0))
    a, tail = pl.pallas_call(
        functools.partial(_conv_a_kernel, width=width),
        grid=(batch, nt),
        in_specs=[
            pl.BlockSpec((None, tt, ch), lambda bi, t: (0, bi * nt + t, 0)),
            pl.BlockSpec((None, tt, ch), lambda bi, t: (1, bi * nt + t, 0)),
            pl.BlockSpec((1, CONV_HALO, ch), lambda bi, t: (bi, 0, 0)),
            pl.BlockSpec((width, ch), lambda bi, t: (0, 0)),
            vec(), vec(), vec(),
        ],
        out_specs=[
            pl.BlockSpec((tt, ch), lambda bi, t: (bi * nt + t, 0)),
            pl.BlockSpec((1, CONV_HALO, ch), lambda bi, t: (bi, 0, 0)),
        ],
        out_shape=[jax.ShapeDtypeStruct((m, ch), BF16), jax.ShapeDtypeStruct((batch, CONV_HALO, ch), F32)],
        scratch_shapes=[
            pltpu.VMEM((CONV_HALO + tt, ch), F32),
            pltpu.VMEM((SUBLANES - 1, CONV_HALO + tt - SUBLANES, ch), F32),
            pltpu.VMEM((tt, ch), F32),
        ],
        compiler_params=_params("parallel", "arbitrary"),
        name="conv_a",
    )(zf, zf, pre, w, b.reshape(1, ch), lg.reshape(1, ch), lb.reshape(1, ch))
    return a, tail[:, CONV_HALO - (width - 1):]


def _conv_a_short_kernel(val_ref, gate_ref, pre_ref, w_ref, b_ref, lg_ref, lb_ref, a_ref, g_ref, *, width):
    steps = a_ref.shape[0]
    g_ref[...] = val_ref[...] * jax.nn.sigmoid(gate_ref[...])
    for t in range(steps):
        acc = jnp.zeros(a_ref.shape[1:], F32)
        for j in range(width):
            i = t + j - (width - 1)
            acc = acc + w_ref[j:j + 1, :] * (g_ref[i] if i >= 0 else pre_ref[i + width - 1])
        a_ref[t] = _ln_silu(acc + b_ref[...], lg_ref[...], lb_ref[...]).astype(BF16)


def _conv_a_short(val, gate, pre, w, b, lg, lb):
    steps, batch, ch = val.shape
    width = w.shape[0]
    assert pre.shape[0] == width - 1
    return pl.pallas_call(
        functools.partial(_conv_a_short_kernel, width=width),
        out_shape=[jax.ShapeDtypeStruct((steps, batch, ch), BF16), jax.ShapeDtypeStruct((steps, batch, ch), F32)],
        compiler_params=pltpu.CompilerParams(vmem_limit_bytes=VMEM_LIMIT),
        name="conv_a_short",
    )(val, gate, pre, w, b.reshape(1, ch), lg.reshape(1, ch), lb.reshape(1, ch))


def _lambda(lp):
    e1 = jnp.exp(jnp.sum(lp[0:1] * lp[1:2], axis=-1, keepdims=True))
    e2 = jnp.exp(jnp.sum(lp[2:3] * lp[3:4], axis=-1, keepdims=True))
    return e1 - e2 + LAMBDA_INIT


def _softmax_update(s, v, idx, m_ref, l_ref, acc_ref):
    m_prev = m_ref[idx]
    m_new = jnp.maximum(m_prev, jnp.max(s, axis=-1, keepdims=True))
    alpha = jnp.exp(m_prev - m_new)
    p = jnp.exp(s - m_new[:, :1])
    l_ref[idx] = alpha * l_ref[idx] + jnp.sum(p, axis=-1, keepdims=True)
    acc_ref[idx] = alpha * acc_ref[idx] + jnp.dot(p.astype(BF16), v, preferred_element_type=F32)
    m_ref[idx] = m_new


def _head_norm(o, g):
    return o * lax.rsqrt(jnp.mean(o * o, axis=-1, keepdims=True) + RMS_EPS) * g * (1.0 - LAMBDA_INIT)


_NT = (((1,), (1,)), ((), ()))


def _attn_prompt_kernel(lam_ref, g_ref, q_ref, k_ref, v_ref, o_ref, m_ref, l_ref, acc_ref, *, n_heads):
    qi = pl.program_id(1)
    ki = pl.program_id(2)
    tq = q_ref.shape[0]
    tk = k_ref.shape[0]
    hd = q_ref.shape[1] // n_heads

    @pl.when(ki == 0)
    def _():
        m_ref[...] = jnp.full(m_ref.shape, NEG_BIG, F32)
        l_ref[...] = jnp.zeros(l_ref.shape, F32)
        acc_ref[...] = jnp.zeros(acc_ref.shape, F32)

    def step(diagonal):
        rel = (lax.broadcasted_iota(jnp.int32, (tq, tk), 1) - lax.broadcasted_iota(jnp.int32, (tq, tk), 0)
               + (ki * tk - qi * tq)).astype(F32)
        lane = lax.broadcasted_iota(jnp.int32, (tq, hd), 1)
        for h in range(n_heads):
            cols = slice(h * hd, (h + 1) * hd)
            qh = q_ref[:, cols]
            kh = k_ref[:, cols]
            vh = v_ref[:, cols]
            bias = (2.0 ** -(8.0 * (h + 1) / n_heads)) * rel
            if diagonal:
                bias = jnp.where(rel <= 0, bias, NEG_BIG)
            for mp in range(2):
                half = (lane < hd // 2) if mp == 0 else (lane >= hd // 2)
                qm = jnp.where(half, qh, jnp.zeros_like(qh))
                s = lax.dot_general(qm, kh, _NT, preferred_element_type=F32) + bias
                _softmax_update(s, vh, 2 * h + mp, m_ref, l_ref, acc_ref)

    @pl.when(ki < qi)
    def _():
        step(False)

    @pl.when(ki == qi)
    def _():
        step(True)
        lam = _lambda(lam_ref[...])
        for h in range(n_heads):
            o = acc_ref[2 * h] / l_ref[2 * h] - lam * (acc_ref[2 * h + 1] / l_ref[2 * h + 1])
            o_ref[:, h * hd:(h + 1) * hd] = _head_norm(o, g_ref[...]).astype(BF16)


def _attn_prompt(q, k, v, lam_params, subln_g, batch, n_heads):
    m, ad = q.shape
    seq = m // batch
    hd = ad // n_heads
    tq = _row_tile(seq, ATTN_TILE)
    nq = seq // tq
    kv_spec = pl.BlockSpec((tq, ad), lambda b, qi, ki: (b * nq + jnp.minimum(ki, qi), 0))
    return pl.pallas_call(
        functools.partial(_attn_prompt_kernel, n_heads=n_heads),
        grid=(batch, nq, nq),
        in_specs=[
            pl.BlockSpec(lam_params.shape, lambda b, qi, ki: (0, 0)),
            pl.BlockSpec((1, hd), lambda b, qi, ki: (0, 0)),
            pl.BlockSpec((tq, ad), lambda b, qi, ki: (b * nq + qi, 0)),
            kv_spec, kv_spec,
        ],
        out_specs=pl.BlockSpec((tq, ad), lambda b, qi, ki: (b * nq + qi, 0)),
        out_shape=jax.ShapeDtypeStruct((m, ad), BF16),
        scratch_shapes=[pltpu.VMEM((2 * n_heads, tq, hd), F32)] * 3,
        compiler_params=_params("parallel", "parallel", "arbitrary"),
        name="attn_prompt",
    )(lam_params, subln_g.reshape(1, hd), q, k, v)


def _attn_sample_kernel(pt_ref, lam_ref, g_ref, q_ref, kn_ref, vn_ref, *rest, pp, past, n_heads):
    k_pages = rest[:pp]
    v_pages = rest[pp:2 * pp]
    o_ref, m_ref, l_ref, acc_ref, qm_ref, bias_ref = rest[2 * pp:]
    st = pl.program_id(1)
    r = q_ref.shape[1]
    hd = q_ref.shape[2]
    page_tokens = k_pages[0].shape[1]
    pc = page_tokens * n_heads
    hshift = n_heads.bit_length() - 1
    assert 1 << hshift == n_heads
    row = lax.broadcasted_iota(jnp.int32, (2 * r, 1), 0)
    row_head = row & (n_heads - 1)
    row_step = (row & (r - 1)) >> hshift
    slope = jnp.zeros((2 * r, 1), F32)
    for h in range(n_heads):
        slope = jnp.where(row_head == h, 2.0 ** -(8.0 * (h + 1) / n_heads), slope)

    @pl.when(st == 0)
    def _():
        m_ref[...] = jnp.full(m_ref.shape, NEG_BIG, F32)
        l_ref[...] = jnp.zeros(l_ref.shape, F32)
        acc_ref[...] = jnp.zeros(acc_ref.shape, F32)
        q = q_ref[0]
        lane = lax.broadcasted_iota(jnp.int32, q.shape, 1)
        zero = jnp.zeros_like(q)
        qm_ref[...] = jnp.concatenate(
            [jnp.where(lane < hd // 2, q, zero), jnp.where(lane >= hd // 2, q, zero)], axis=0)
        col = lax.broadcasted_iota(jnp.int32, (1, pc), 1)
        rel0 = ((col >> hshift) - row_step - past).astype(F32)
        bias_ref[...] = jnp.where((col & (n_heads - 1)) == row_head, slope * rel0, NEG_BIG)

    qm = qm_ref[...]

    def update(tiles):
        m_prev = m_ref[...]
        m_cur = None
        for s, c, _ in tiles:
            m_t = jnp.max(s, axis=-1, keepdims=True) + c
            m_cur = m_t if m_cur is None else jnp.maximum(m_cur, m_t)
        m_new = jnp.maximum(m_prev, m_cur)
        alpha = jnp.exp(m_prev - m_new)
        lsum = alpha * l_ref[...]
        acc = alpha * acc_ref[...]
        for s, c, v in tiles:
            p = jnp.exp(s - (m_new[:, :1] - c))
            lsum = lsum + jnp.sum(p, axis=-1, keepdims=True)
            acc = acc + jnp.dot(p.astype(BF16), v, preferred_element_type=F32)
        l_ref[...] = lsum
        acc_ref[...] = acc
        m_ref[...] = m_new

    tiles = []
    for p in range(pp):
        kp = k_pages[p][0].reshape(pc, hd).astype(BF16)
        base = ((st * pp + p) * page_tokens).astype(F32)
        s = lax.dot_general(qm, kp, _NT, preferred_element_type=F32) + bias_ref[...]
        tiles.append((s, slope * base, v_pages[p][0].reshape(pc, hd).astype(BF16)))
    update(tiles)

    @pl.when(st == pl.num_programs(1) - 1)
    def _():
        ncol = lax.broadcasted_iota(jnp.int32, (1, r), 1)
        dist = ((ncol >> hshift) - row_step)
        ok = ((ncol & (n_heads - 1)) == row_head) & (dist <= 0)
        s = lax.dot_general(qm, kn_ref[0], _NT, preferred_element_type=F32) + slope * dist.astype(F32)
        update([(jnp.where(ok, s, NEG_BIG), jnp.zeros((2 * r, 1), F32), vn_ref[0])])
        lam = _lambda(lam_ref[...])
        o = acc_ref[0:r] / l_ref[0:r] - lam * (acc_ref[r:2 * r] / l_ref[r:2 * r])
        o_ref[0] = _head_norm(o, g_ref[...]).astype(BF16)


def _attn_sample(q, kn, vn, cache_k, cache_v, page_table, lam_params, subln_g, n_heads):
    batch, r, hd = q.shape
    n_pages = page_table.shape[1]
    page_tokens = cache_k.shape[1]
    pp = PAGES_PER_STEP
    while n_pages % pp:
        pp -= 1
    assert r & (r - 1) == 0
    page_specs = [
        pl.BlockSpec((1,) + cache_k.shape[1:], functools.partial(lambda b, s, pt, p: (pt[b, s * pp + p], 0, 0, 0), p=p))
        for p in range(pp)
    ]
    new_spec = pl.BlockSpec((1, r, hd), lambda b, s, pt: (b, 0, 0))
    grid_spec = pltpu.PrefetchScalarGridSpec(
        num_scalar_prefetch=1,
        grid=(batch, n_pages // pp),
        in_specs=[
            pl.BlockSpec(lam_params.shape, lambda b, s, pt: (0, 0)),
            pl.BlockSpec((1, hd), lambda b, s, pt: (0, 0)),
            new_spec, new_spec, new_spec,
        ] + page_specs + page_specs,
        out_specs=new_spec,
        scratch_shapes=[pltpu.VMEM((2 * r, hd), F32)] * 3 + [
            pltpu.VMEM((2 * r, hd), BF16),
            pltpu.VMEM((2 * r, page_tokens * n_heads), F32),
        ],
    )
    return pl.pallas_call(
        functools.partial(_attn_sample_kernel, pp=pp, past=n_pages * page_tokens, n_heads=n_heads),
        grid_spec=grid_spec,
        out_shape=jax.ShapeDtypeStruct((batch, r, hd), BF16),
        compiler_params=_params("parallel", "arbitrary"),
        name="attn_sample",
    )(page_table, lam_params, subln_g.reshape(1, hd), q, kn, vn,
      *([cache_k] * pp), *([cache_v] * pp))


def _out_ab_kernel(x_ref, a_ref, att_ref, w_ref, o_ref):
    ca = a_ref.shape[1]
    o_ref[...] = (x_ref[...]
                  + jnp.dot(a_ref[...], w_ref[0:ca, :], preferred_element_type=F32)
                  + jnp.dot(att_ref[...], w_ref[ca:, :], preferred_element_type=F32))


def _out_ab(x, a, att, w):
    m, d = x.shape
    tm = _row_tile(m, ROW_TILE)
    return pl.pallas_call(
        _out_ab_kernel,
        grid=(m // tm,),
        in_specs=[
            pl.BlockSpec((tm, d), lambda i: (i, 0)),
            pl.BlockSpec((tm, a.shape[1]), lambda i: (i, 0)),
            pl.BlockSpec((tm, att.shape[1]), lambda i: (i, 0)),
            pl.BlockSpec(w.shape, lambda i: (0, 0)),
        ],
        out_specs=pl.BlockSpec((tm, d), lambda i: (i, 0)),
        out_shape=jax.ShapeDtypeStruct((m, d), F32),
        compiler_params=_params("parallel"),
        name="out_ab",
    )(x, a, att, w)


def _mixer_c_kernel(x_ref, gn_ref, wb_ref, wc_ref, wh_ref, cw_ref, wo_ref, *rest, width, tiles_per_seq, rows_per_seq):
    short = rows_per_seq is not None
    if short:
        hist_ref, o_ref, u_ref, xn_ref, ubuf_ref = rest
    else:
        o_ref, tail_ref, xn_ref, ubuf_ref, carry_ref = rest
    i = pl.program_id(0)
    c = pl.program_id(1)
    tm = x_ref.shape[0]
    tc = wb_ref.shape[1]

    @pl.when(c == 0)
    def _():
        x = x_ref[...]
        xn_ref[...] = _rmsnorm(x, gn_ref[...]).astype(BF16)
        o_ref[...] = x
        if not short:
            @pl.when(i == 0)
            def _():
                carry_ref[...] = jnp.zeros(carry_ref.shape, F32)

    xn = xn_ref[...]
    gate_b = jnp.dot(xn, wb_ref[...], preferred_element_type=F32)
    u = jnp.dot(xn, wc_ref[...], preferred_element_type=F32) * jnp.dot(xn, wh_ref[...], preferred_element_type=F32)
    ubuf_ref[SUBLANES:SUBLANES + tm, :] = u
    if short:
        u_ref[...] = u
        ubuf_ref[0:SUBLANES, :] = jnp.zeros((SUBLANES, tc), F32)
    else:
        tail_ref[0] = u[tm - SUBLANES:tm, :]
        ubuf_ref[0:SUBLANES, :] = jnp.where(i % tiles_per_seq == 0, 0.0, carry_ref[c])
        carry_ref[c] = u[tm - SUBLANES:tm, :]
    acc = cw_ref[width - 1:width, :] * u
    for j in range(1, width):
        prev = ubuf_ref[SUBLANES - j:SUBLANES - j + tm, :]
        if short:
            pos = lax.broadcasted_iota(jnp.int32, (tm, 1), 0) & (rows_per_seq - 1)
            prev = jnp.where(pos >= j, prev, hist_ref[j - 1])
        acc = acc + cw_ref[width - 1 - j:width - j, :] * prev
    y = (gate_b * acc).astype(BF16)
    o_ref[...] += jnp.dot(y, wo_ref[...], preferred_element_type=F32)


def _mixer_c(x, gn, w_in, conv_w, w_out, batch, prefix):
    m, d = x.shape
    ch = w_out.shape[0]
    width = conv_w.shape[0]
    assert w_in.shape[1] == 3 * ch and width - 1 <= SUBLANES
    seq = m // batch
    tm = _row_tile(m, ROW_TILE)
    tc = _row_tile(ch, MIXC_TILE)
    nc = ch // tc
    short = seq < tm
    in_specs = [
        pl.BlockSpec((tm, d), lambda i, c: (i, 0)),
        pl.BlockSpec((1, d), lambda i, c: (0, 0)),
        pl.BlockSpec((d, tc), lambda i, c: (0, c)),
        pl.BlockSpec((d, tc), lambda i, c: (0, nc + c)),
        pl.BlockSpec((d, tc), lambda i, c: (0, 2 * nc + c)),
        pl.BlockSpec((width, tc), lambda i, c: (0, c)),
        pl.BlockSpec((tc, d), lambda i, c: (c, 0)),
    ]
    args = [x, gn.reshape(1, d), w_in, w_in, w_in, conv_w, w_out]
    o_spec = pl.BlockSpec((tm, d), lambda i, c: (i, 0))
    o_shape = jax.ShapeDtypeStruct((m, d), F32)
    scratch = [pltpu.VMEM((tm, d), BF16), pltpu.VMEM((SUBLANES + tm, tc), F32)]
    if short:
        assert tm % seq == 0 and seq & (seq - 1) == 0
        padded = jnp.concatenate([prefix, jnp.zeros((batch, seq, ch), F32)], axis=1)
        hist = jnp.stack([padded[:, width - 1 - j:width - 1 - j + seq].reshape(m, ch) for j in range(1, width)])
        in_specs.append(pl.BlockSpec((width - 1, tm, tc), lambda i, c: (0, i, c)))
        args.append(hist)
        out_specs = [o_spec, pl.BlockSpec((tm, tc), lambda i, c: (i, c))]
        out_shape = [o_shape, jax.ShapeDtypeStruct((m, ch), F32)]
        kern = functools.partial(_mixer_c_kernel, width=width, tiles_per_seq=None, rows_per_seq=seq)
    else:
        assert seq % tm == 0 and prefix is None
        out_specs = [o_spec, pl.BlockSpec((1, SUBLANES, tc), lambda i, c: (i, 0, c))]
        out_shape = [o_shape, jax.ShapeDtypeStruct((m // tm, SUBLANES, ch), F32)]
        scratch.append(pltpu.VMEM((nc, SUBLANES, tc), F32))
        kern = functools.partial(_mixer_c_kernel, width=width, tiles_per_seq=seq // tm, rows_per_seq=None)
    o, aux = pl.pallas_call(
        kern,
        grid=(m // tm, nc),
        in_specs=in_specs,
        out_specs=out_specs,
        out_shape=out_shape,
        scratch_shapes=scratch,
        compiler_params=_params("arbitrary", "arbitrary"),
        name="mixer_c",
    )(*args)
    if short:
        u_hist = jnp.concatenate([prefix, aux.reshape(batch, seq, ch)], axis=1)
        state = u_hist[:, -(width - 1):]
    else:
        tails = aux.reshape(batch, seq // tm, SUBLANES, ch)
        state = tails[:, -1, SUBLANES - (width - 1):, :]
    return o, state


def _forward(x3, w, conv_a_prefix, conv_c_prefix, attend):
    batch, seq, d = x3.shape
    m = batch * seq
    x = x3.reshape(m, d)
    seg = w["conv_a_w"].shape[1]
    hd = w["diff_subln_g"].shape[0]
    n_heads = seg // hd
    width_a = w["conv_a_w"].shape[0]

    def ffn(x, which, layer, final_norm=False):
        return _ffn(x, w["norm_" + which], w[which + "_w_gate"], w[which + "_w_up"], w[which + "_w_down"],
                    w["final_norm"], layer, final_norm=final_norm)

    x = ffn(x, "ffn1", 0)
    zf, zb = _proj_ab(x, w["norm_mix"][0], w["w_in_ab"], seg, (hd // 2) ** -0.5)
    conv_args = (w["conv_a_w"], w["conv_a_b"], w["conv_a_ln_g"], w["conv_a_ln_b"])
    if seq >= CONV_HALO:
        assert conv_a_prefix is None
        a, conv_a_state = _conv_a(zf, jnp.zeros((batch, width_a - 1, seg), F32), *conv_args, batch)
    else:
        tm = lambda t: jnp.swapaxes(t.reshape(batch, seq, seg), 0, 1)
        a, g = _conv_a_short(tm(zf[0]), tm(zf[1]), jnp.swapaxes(conv_a_prefix, 0, 1), *conv_args)
        a = jnp.swapaxes(a, 0, 1).reshape(m, seg)
        conv_a_state = jnp.concatenate([conv_a_prefix, jnp.swapaxes(g, 0, 1)], axis=1)[:, -(width_a - 1):]
    att = attend(zb[0], zb[1], zb[2], batch, seq, n_heads, hd)
    x = _out_ab(x, a, att, w["w_out_ab"])
    x = ffn(x, "ffn2", 0)

    x = ffn(x, "ffn1", 1)
    x, conv_c_state = _mixer_c(x, w["norm_mix"][1], w["w_in_c"], w["conv_c_w"], w["w_out_c"], batch, conv_c_prefix)
    y = ffn(x, "ffn2", 1, final_norm=True)
    return (y.reshape(batch, seq, d), zf[2].reshape(batch, seq, n_heads, hd), zf[3].reshape(batch, seq, n_heads, hd),
            conv_a_state, conv_c_state)


def kernel(x_prompt, x_sample, cache_k, cache_v, state_conv_a, state_conv_c, page_table, norm_ffn1, ffn1_w_gate, ffn1_w_up, ffn1_w_down, norm_mix, norm_ffn2, ffn2_w_gate, ffn2_w_up, ffn2_w_down, w_in_ab, conv_a_w, conv_a_b, conv_a_ln_g, conv_a_ln_b, diff_lambda, diff_subln_g, w_out_ab, w_in_c, conv_c_w, w_out_c, final_norm):
    w = dict(
        norm_ffn1=norm_ffn1, norm_mix=norm_mix, norm_ffn2=norm_ffn2, final_norm=final_norm,
        ffn1_w_gate=ffn1_w_gate.astype(BF16), ffn1_w_up=ffn1_w_up.astype(BF16), ffn1_w_down=ffn1_w_down.astype(BF16),
        ffn2_w_gate=ffn2_w_gate.astype(BF16), ffn2_w_up=ffn2_w_up.astype(BF16), ffn2_w_down=ffn2_w_down.astype(BF16),
        w_in_ab=w_in_ab.astype(BF16), w_out_ab=w_out_ab.astype(BF16),
        w_in_c=w_in_c.astype(BF16), w_out_c=w_out_c.astype(BF16),
        conv_a_w=conv_a_w, conv_a_b=conv_a_b, conv_a_ln_g=conv_a_ln_g, conv_a_ln_b=conv_a_ln_b,
        conv_c_w=conv_c_w, diff_subln_g=diff_subln_g,
    )

    def attend_prompt(q, kb, vb, batch, seq, n_heads, hd):
        return _attn_prompt(q, kb, vb, diff_lambda, diff_subln_g, batch, n_heads)

    def attend_sample(q, kb, vb, batch, seq, n_heads, hd):
        rows = lambda t: t.reshape(batch, seq * n_heads, hd)
        att = _attn_sample(rows(q), rows(kb), rows(vb), cache_k, cache_v, page_table,
                           diff_lambda, diff_subln_g, n_heads)
        return att.reshape(batch * seq, n_heads * hd)

    y_p, k_p, v_p, ca_p, cc_p = _forward(x_prompt, w, None, None, attend_prompt)
    y_s, k_s, v_s, ca_s, cc_s = _forward(x_sample, w, state_conv_a, state_conv_c, attend_sample)
    return (y_p, y_s, k_p, v_p, ca_p, cc_p, k_s, v_s, ca_s, cc_s)
```

```python
import functools
import math

import jax
import jax.numpy as jnp
from jax import lax
from jax.experimental import pallas as pl
from jax.experimental.pallas import tpu as pltpu

F32 = jnp.float32
BF16 = jnp.bfloat16

RMS_EPS = 1e-6
LN_EPS = 1e-5
LAMBDA_INIT = 0.8 - 0.6 * math.exp(-0.3 * 0)
NEG_BIG = -1e30

V7X_VMEM_BYTES = 64 * 1024 * 1024
VMEM_LIMIT = V7X_VMEM_BYTES - 8 * 1024 * 1024
SUBLANES = 8
LANES = 128

ROW_TILE = 512
WIDE_ROW_TILE = 1024
FF_TILE = 512
MIXC_TILE = 512
CONV_TIME_TILE = 256
CONV_HALO = 32
ATTN_TILE = 512
PAGES_PER_STEP = 8


def _row_tile(m, target):
    t = min(m, target)
    while m % t or t % SUBLANES:
        t -= 1
    return t


def _params(*sem):
    return pltpu.CompilerParams(dimension_semantics=sem, vmem_limit_bytes=VMEM_LIMIT)


def _rmsnorm(x, g):
    return x * lax.rsqrt(jnp.mean(x * x, axis=-1, keepdims=True) + RMS_EPS) * g


def _silu(x):
    return x * jax.nn.sigmoid(x)


def _ffn_kernel(x_ref, gn_ref, wg_ref, wu_ref, wd_ref, fn_ref, o_ref, xn_ref, *, final_norm):
    f = pl.program_id(1)

    @pl.when(f == 0)
    def _():
        x = x_ref[...]
        xn_ref[...] = _rmsnorm(x, gn_ref[...]).astype(BF16)
        o_ref[...] = x

    xn = xn_ref[...]
    g = jnp.dot(xn, wg_ref[...], preferred_element_type=F32)
    u = jnp.dot(xn, wu_ref[...], preferred_element_type=F32)
    h = (_silu(g) * (0.5 * u)).astype(BF16)
    o_ref[...] += jnp.dot(h, wd_ref[...], preferred_element_type=F32)

    if final_norm:
        @pl.when(f == pl.num_programs(1) - 1)
        def _():
            o_ref[...] = _rmsnorm(o_ref[...], fn_ref[...])


def _ffn(x, gn, wg, wu, wd, fn, layer, *, final_norm):
    m, d = x.shape
    ff = wg.shape[2]
    tm = _row_tile(m, WIDE_ROW_TILE)
    tf = _row_tile(ff, FF_TILE)
    return pl.pallas_call(
        functools.partial(_ffn_kernel, final_norm=final_norm),
        grid=(m // tm, ff // tf),
        in_specs=[
            pl.BlockSpec((tm, d), lambda i, f: (i, 0)),
            pl.BlockSpec((None, 1, d), lambda i, f: (layer, 0, 0)),
            pl.BlockSpec((None, d, tf), lambda i, f: (layer, 0, f)),
            pl.BlockSpec((None, d, tf), lambda i, f: (layer, 0, f)),
            pl.BlockSpec((None, tf, d), lambda i, f: (layer, f, 0)),
            pl.BlockSpec((1, d), lambda i, f: (0, 0)),
        ],
        out_specs=pl.BlockSpec((tm, d), lambda i, f: (i, 0)),
        out_shape=jax.ShapeDtypeStruct((m, d), F32),
        scratch_shapes=[pltpu.VMEM((tm, d), BF16)],
        compiler_params=_params("parallel", "arbitrary"),
        name="ffn",
    )(x, gn.reshape(gn.shape[0], 1, d), wg, wu, wd, fn.reshape(1, d))


def _proj_ab_kernel(x_ref, gn_ref, w_ref, zf_ref, zb_ref, xn_ref, *, q_scale):
    j = pl.program_id(1)

    @pl.when(j == 0)
    def _():
        xn_ref[...] = _rmsnorm(x_ref[...], gn_ref[...]).astype(BF16)

    z = jnp.dot(xn_ref[...], w_ref[...], preferred_element_type=F32)
    zf_ref[...] = z
    zb_ref[...] = (z * jnp.where(j == 0, q_scale, 1.0)).astype(BF16)


def _proj_ab(x, gn, w, seg, q_scale):
    m, d = x.shape
    assert w.shape[1] == 5 * seg
    tm = _row_tile(m, WIDE_ROW_TILE)
    w_col = lambda j: jnp.where(j == 0, 2, jnp.where(j <= 2, j - 1, j))
    zb_blk = lambda j: jnp.where(j == 0, 0, jnp.where(j <= 3, 1, 2))
    return pl.pallas_call(
        functools.partial(_proj_ab_kernel, q_scale=q_scale),
        grid=(m // tm, 5),
        in_specs=[
            pl.BlockSpec((tm, d), lambda i, j: (i, 0)),
            pl.BlockSpec((1, d), lambda i, j: (0, 0)),
            pl.BlockSpec((d, seg), lambda i, j: (0, w_col(j))),
        ],
        out_specs=[
            pl.BlockSpec((None, tm, seg), lambda i, j: (jnp.maximum(j - 1, 0), i, 0)),
            pl.BlockSpec((None, tm, seg), lambda i, j: (zb_blk(j), i, 0)),
        ],
        out_shape=[jax.ShapeDtypeStruct((4, m, seg), F32), jax.ShapeDtypeStruct((3, m, seg), BF16)],
        scratch_shapes=[pltpu.VMEM((tm, d), BF16)],
        compiler_params=_params("parallel", "arbitrary"),
        name="proj_ab",
    )(x, gn.reshape(1, d), w)


def _ln_silu(a, lg, lb):
    mu = jnp.mean(a, axis=-1, keepdims=True)
    c = a - mu
    var = jnp.mean(c * c, axis=-1, keepdims=True)
    return _silu(c * lax.rsqrt(var + LN_EPS) * lg + lb)


def _conv_a_kernel(val_ref, gate_ref, pre_ref, w_ref, b_ref, lg_ref, lb_ref, a_ref, tail_ref,
                   buf_ref, sh_ref, c_ref, *, width):
    t = pl.program_id(1)
    tt, ch = val_ref.shape

    @pl.when(t == 0)
    def _():
        buf_ref[0:CONV_HALO, :] = pre_ref[0]

    @pl.when(t > 0)
    def _():
        buf_ref[0:CONV_HALO, :] = buf_ref[tt:tt + CONV_HALO, :]

    buf_ref[CONV_HALO:CONV_HALO + tt, :] = val_ref[...] * jax.nn.sigmoid(gate_ref[...])
    tail_ref[0] = buf_ref[tt:tt + CONV_HALO, :]
    n_sh = sh_ref.shape[1]
    for s in range(1, SUBLANES):
        sh_ref[s - 1] = buf_ref[s:s + n_sh, :]
    off = CONV_HALO - (width - 1)
    cc, rr = 2 * LANES, 8 * SUBLANES
    for c0 in range(0, ch, cc):
        for r0 in range(0, tt, rr):
            acc = jnp.zeros((rr, cc), F32)
            for j in range(width):
                s = (off + j) % SUBLANES
                a0 = off + j - s + r0
                if s == 0:
                    win = buf_ref[a0:a0 + rr, c0:c0 + cc]
                else:
                    win = sh_ref[s - 1, a0:a0 + rr, c0:c0 + cc]
                acc = acc + w_ref[j:j + 1, c0:c0 + cc] * win
            c_ref[r0:r0 + rr, c0:c0 + cc] = acc + b_ref[:, c0:c0 + cc]
    ln_rows = 8 * SUBLANES
    for r0 in range(0, tt, ln_rows):
        y = _ln_silu(c_ref[r0:r0 + ln_rows, :], lg_ref[...], lb_ref[...])
        a_ref[r0:r0 + ln_rows, :] = y.astype(BF16)


def _conv_a(zf, prefix, w, b, lg, lb, batch):
    _, m, ch = zf.shape
    seq = m // batch
    width = w.shape[0]
    assert width - 1 <= CONV_HALO
    tt = _row_tile(seq, CONV_TIME_TILE)
    assert tt >= CONV_HALO and tt % (8 * SUBLANES) == 0
    nt = seq // tt
    pre = jnp.pad(prefix, ((0, 0), (CONV_HALO - (width - 1), 0), (0, 0)))
    vec = lambda: pl.BlockSpec((1, ch), lambda bi, t: (0, 0))
    a, tail = pl.pallas_call(
        functools.partial(_conv_a_kernel, width=width),
        grid=(batch, nt),
        in_specs=[
            pl.BlockSpec((None, tt, ch), lambda bi, t: (0, bi * nt + t, 0)),
            pl.BlockSpec((None, tt, ch), lambda bi, t: (1, bi * nt + t, 0)),
            pl.BlockSpec((1, CONV_HALO, ch), lambda bi, t: (bi, 0, 0)),
            pl.BlockSpec((width, ch), lambda bi, t: (0, 0)),
            vec(), vec(), vec(),
        ],
        out_specs=[
            pl.BlockSpec((tt, ch), lambda bi, t: (bi * nt + t, 0)),
            pl.BlockSpec((1, CONV_HALO, ch), lambda bi, t: (bi, 0, 0)),
        ],
        out_shape=[jax.ShapeDtypeStruct((m, ch), BF16), jax.ShapeDtypeStruct((batch, CONV_HALO, ch), F32)],
        scratch_shapes=[
            pltpu.VMEM((CONV_HALO + tt, ch), F32),
            pltpu.VMEM((SUBLANES - 1, CONV_HALO + tt - SUBLANES, ch), F32),
            pltpu.VMEM((tt, ch), F32),
        ],
        compiler_params=_params("parallel", "arbitrary"),
        name="conv_a",
    )(zf, zf, pre, w, b.reshape(1, ch), lg.reshape(1, ch), lb.reshape(1, ch))
    return a, tail[:, CONV_HALO - (width - 1):]


def _conv_a_short_kernel(val_ref, gate_ref, pre_ref, w_ref, b_ref, lg_ref, lb_ref, a_ref, g_ref, *, width):
    steps = a_ref.shape[0]
    g_ref[...] = val_ref[...] * jax.nn.sigmoid(gate_ref[...])
    for t in range(steps):
        acc = jnp.zeros(a_ref.shape[1:], F32)
        for j in range(width):
            i = t + j - (width - 1)
            acc = acc + w_ref[j:j + 1, :] * (g_ref[i] if i >= 0 else pre_ref[i + width - 1])
        a_ref[t] = _ln_silu(acc + b_ref[...], lg_ref[...], lb_ref[...]).astype(BF16)


def _conv_a_short(val, gate, pre, w, b, lg, lb):
    steps, batch, ch = val.shape
    width = w.shape[0]
    assert pre.shape[0] == width - 1
    return pl.pallas_call(
        functools.partial(_conv_a_short_kernel, width=width),
        out_shape=[jax.ShapeDtypeStruct((steps, batch, ch), BF16), jax.ShapeDtypeStruct((steps, batch, ch), F32)],
        compiler_params=pltpu.CompilerParams(vmem_limit_bytes=VMEM_LIMIT),
        name="conv_a_short",
    )(val, gate, pre, w, b.reshape(1, ch), lg.reshape(1, ch), lb.reshape(1, ch))


def _lambda(lp):
    e1 = jnp.exp(jnp.sum(lp[0:1] * lp[1:2], axis=-1, keepdims=True))
    e2 = jnp.exp(jnp.sum(lp[2:3] * lp[3:4], axis=-1, keepdims=True))
    return e1 - e2 + LAMBDA_INIT


def _head_norm(o, g):
    return o * lax.rsqrt(jnp.mean(o * o, axis=-1, keepdims=True) + RMS_EPS) * g * (1.0 - LAMBDA_INIT)


_NT = (((1,), (1,)), ((), ()))
_TN = (((0,), (0,)), ((), ()))


def _attn_prompt_kernel(lam_ref, g_ref, q_ref, k_ref, v_ref, o_ref, m_ref, l_ref, acc_ref, *, n_heads):
    qi = pl.program_id(1)
    ki = pl.program_id(2)
    tq = q_ref.shape[0]
    tk = k_ref.shape[0]
    hd = q_ref.shape[1] // n_heads

    @pl.when(ki == 0)
    def _():
        m_ref[...] = jnp.full(m_ref.shape, NEG_BIG, F32)
        l_ref[...] = jnp.zeros(l_ref.shape, F32)
        acc_ref[...] = jnp.zeros(acc_ref.shape, F32)

    def step(diagonal):
        rel = (lax.broadcasted_iota(jnp.int32, (tk, tq), 0) - lax.broadcasted_iota(jnp.int32, (tk, tq), 1)
               + (ki * tk - qi * tq)).astype(F32)
        lane = lax.broadcasted_iota(jnp.int32, (tq, hd), 1)
        for h in range(n_heads):
            cols = slice(h * hd, (h + 1) * hd)
            qh = q_ref[:, cols]
            kh = k_ref[:, cols]
            vh = v_ref[:, cols]
            bias = (2.0 ** -(8.0 * (h + 1) / n_heads)) * rel
            if diagonal:
                bias = jnp.where(rel <= 0, bias, NEG_BIG)
            for mp in range(2):
                half = (lane < hd // 2) if mp == 0 else (lane >= hd // 2)
                qm = jnp.where(half, qh, jnp.zeros_like(qh))
                s = lax.dot_general(kh, qm, _NT, preferred_element_type=F32) + bias
                idx = 2 * h + mp
                m_prev = m_ref[idx]
                m_new = jnp.maximum(m_prev, jnp.max(s, axis=0, keepdims=True))
                alpha = jnp.exp(m_prev - m_new)
                p = jnp.exp(s - m_new)
                l_ref[idx] = alpha * l_ref[idx] + jnp.sum(p, axis=0, keepdims=True)
                pv = lax.dot_general(vh, p.astype(BF16), _TN, preferred_element_type=F32)
                acc_ref[idx] = alpha * acc_ref[idx] + pv
                m_ref[idx] = m_new

    @pl.when(ki < qi)
    def _():
        step(False)

    @pl.when(ki == qi)
    def _():
        step(True)
        lam = _lambda(lam_ref[...])
        for h in range(n_heads):
            o = (acc_ref[2 * h] * (1.0 / l_ref[2 * h])
                 - lam * (acc_ref[2 * h + 1] * (1.0 / l_ref[2 * h + 1])))
            y = o * lax.rsqrt(jnp.mean(o * o, axis=0, keepdims=True) + RMS_EPS) * g_ref[...] * (1.0 - LAMBDA_INIT)
            o_ref[:, h * hd:(h + 1) * hd] = y.T.astype(BF16)


def _attn_prompt(qkv, lam_params, subln_g, batch, n_heads):
    _, m, ad = qkv.shape
    seq = m // batch
    hd = ad // n_heads
    tq = _row_tile(seq, ATTN_TILE)
    nq = seq // tq
    kv_spec = lambda which: pl.BlockSpec(
        (None, tq, ad), lambda b, qi, ki: (which, b * nq + jnp.minimum(ki, qi), 0))
    return pl.pallas_call(
        functools.partial(_attn_prompt_kernel, n_heads=n_heads),
        grid=(batch, nq, nq),
        in_specs=[
            pl.BlockSpec(lam_params.shape, lambda b, qi, ki: (0, 0)),
            pl.BlockSpec((hd, 1), lambda b, qi, ki: (0, 0)),
            pl.BlockSpec((None, tq, ad), lambda b, qi, ki: (0, b * nq + qi, 0)),
            kv_spec(1), kv_spec(2),
        ],
        out_specs=pl.BlockSpec((tq, ad), lambda b, qi, ki: (b * nq + qi, 0)),
        out_shape=jax.ShapeDtypeStruct((m, ad), BF16),
        scratch_shapes=[
            pltpu.VMEM((2 * n_heads, 1, tq), F32),
            pltpu.VMEM((2 * n_heads, 1, tq), F32),
            pltpu.VMEM((2 * n_heads, hd, tq), F32),
        ],
        compiler_params=_params("parallel", "parallel", "arbitrary"),
        name="attn_prompt",
    )(lam_params, subln_g.reshape(hd, 1), qkv, qkv, qkv)


def _attn_sample_kernel(pt_ref, lam_ref, g_ref, q_ref, kn_ref, vn_ref, *rest, pp, past, n_heads):
    k_pages = rest[:pp]
    v_pages = rest[pp:2 * pp]
    o_ref, m_ref, l_ref, acc_ref, qm_ref, bias_ref = rest[2 * pp:]
    st = pl.program_id(1)
    r = q_ref.shape[1]
    hd = q_ref.shape[2]
    page_tokens = k_pages[0].shape[1]
    pc = page_tokens * n_heads
    hshift = n_heads.bit_length() - 1
    assert 1 << hshift == n_heads
    row = lax.broadcasted_iota(jnp.int32, (2 * r, 1), 0)
    row_head = row & (n_heads - 1)
    row_step = (row & (r - 1)) >> hshift
    slope = jnp.zeros((2 * r, 1), F32)
    for h in range(n_heads):
        slope = jnp.where(row_head == h, 2.0 ** -(8.0 * (h + 1) / n_heads), slope)

    @pl.when(st == 0)
    def _():
        m_ref[...] = jnp.full(m_ref.shape, NEG_BIG, F32)
        l_ref[...] = jnp.zeros(l_ref.shape, F32)
        acc_ref[...] = jnp.zeros(acc_ref.shape, F32)
        q = q_ref[0]
        lane = lax.broadcasted_iota(jnp.int32, q.shape, 1)
        zero = jnp.zeros_like(q)
        qm_ref[...] = jnp.concatenate(
            [jnp.where(lane < hd // 2, q, zero), jnp.where(lane >= hd // 2, q, zero)], axis=0)
        col = lax.broadcasted_iota(jnp.int32, (1, pc), 1)
        rel0 = ((col >> hshift) - row_step - past).astype(F32)
        bias_ref[...] = jnp.where((col & (n_heads - 1)) == row_head, slope * rel0, NEG_BIG)

    qm = qm_ref[...]

    def update(tiles):
        m_prev = m_ref[...]
        m_cur = None
        for s, c, _ in tiles:
            m_t = jnp.max(s, axis=-1, keepdims=True) + c
            m_cur = m_t if m_cur is None else jnp.maximum(m_cur, m_t)
        m_new = jnp.maximum(m_prev, m_cur)
        alpha = jnp.exp(m_prev - m_new)
        lsum = alpha * l_ref[...]
        acc = alpha * acc_ref[...]
        for s, c, v in tiles:
            p = jnp.exp(s - (m_new[:, :1] - c))
            lsum = lsum + jnp.sum(p, axis=-1, keepdims=True)
            acc = acc + jnp.dot(p.astype(BF16), v, preferred_element_type=F32)
        l_ref[...] = lsum
        acc_ref[...] = acc
        m_ref[...] = m_new

    tiles = []
    for p in range(pp):
        kp = k_pages[p][0].reshape(pc, hd).astype(BF16)
        base = ((st * pp + p) * page_tokens).astype(F32)
        s = lax.dot_general(qm, kp, _NT, preferred_element_type=F32) + bias_ref[...]
        tiles.append((s, slope * base, v_pages[p][0].reshape(pc, hd).astype(BF16)))
    update(tiles)

    @pl.when(st == pl.num_programs(1) - 1)
    def _():
        ncol = lax.broadcasted_iota(jnp.int32, (1, r), 1)
        dist = ((ncol >> hshift) - row_step)
        ok = ((ncol & (n_heads - 1)) == row_head) & (dist <= 0)
        s = lax.dot_general(qm, kn_ref[0], _NT, preferred_element_type=F32) + slope * dist.astype(F32)
        update([(jnp.where(ok, s, NEG_BIG), jnp.zeros((2 * r, 1), F32), vn_ref[0])])
        lam = _lambda(lam_ref[...])
        o = acc_ref[0:r] / l_ref[0:r] - lam * (acc_ref[r:2 * r] / l_ref[r:2 * r])
        o_ref[0] = _head_norm(o, g_ref[...]).astype(BF16)


def _attn_sample(q, kn, vn, cache_k, cache_v, page_table, lam_params, subln_g, n_heads):
    batch, r, hd = q.shape
    n_pages = page_table.shape[1]
    page_tokens = cache_k.shape[1]
    pp = PAGES_PER_STEP
    while n_pages % pp:
        pp -= 1
    assert r & (r - 1) == 0
    page_specs = [
        pl.BlockSpec((1,) + cache_k.shape[1:], functools.partial(lambda b, s, pt, p: (pt[b, s * pp + p], 0, 0, 0), p=p))
        for p in range(pp)
    ]
    new_spec = pl.BlockSpec((1, r, hd), lambda b, s, pt: (b, 0, 0))
    grid_spec = pltpu.PrefetchScalarGridSpec(
        num_scalar_prefetch=1,
        grid=(batch, n_pages // pp),
        in_specs=[
            pl.BlockSpec(lam_params.shape, lambda b, s, pt: (0, 0)),
            pl.BlockSpec((1, hd), lambda b, s, pt: (0, 0)),
            new_spec, new_spec, new_spec,
        ] + page_specs + page_specs,
        out_specs=new_spec,
        scratch_shapes=[pltpu.VMEM((2 * r, hd), F32)] * 3 + [
            pltpu.VMEM((2 * r, hd), BF16),
            pltpu.VMEM((2 * r, page_tokens * n_heads), F32),
        ],
    )
    return pl.pallas_call(
        functools.partial(_attn_sample_kernel, pp=pp, past=n_pages * page_tokens, n_heads=n_heads),
        grid_spec=grid_spec,
        out_shape=jax.ShapeDtypeStruct((batch, r, hd), BF16),
        compiler_params=_params("parallel", "arbitrary"),
        name="attn_sample",
    )(page_table, lam_params, subln_g.reshape(1, hd), q, kn, vn,
      *([cache_k] * pp), *([cache_v] * pp))


def _out_ab_kernel(x_ref, a_ref, att_ref, w_ref, o_ref):
    ca = a_ref.shape[1]
    o_ref[...] = (x_ref[...]
                  + jnp.dot(a_ref[...], w_ref[0:ca, :], preferred_element_type=F32)
                  + jnp.dot(att_ref[...], w_ref[ca:, :], preferred_element_type=F32))


def _out_ab(x, a, att, w):
    m, d = x.shape
    tm = _row_tile(m, ROW_TILE)
    return pl.pallas_call(
        _out_ab_kernel,
        grid=(m // tm,),
        in_specs=[
            pl.BlockSpec((tm, d), lambda i: (i, 0)),
            pl.BlockSpec((tm, a.shape[1]), lambda i: (i, 0)),
            pl.BlockSpec((tm, att.shape[1]), lambda i: (i, 0)),
            pl.BlockSpec(w.shape, lambda i: (0, 0)),
        ],
        out_specs=pl.BlockSpec((tm, d), lambda i: (i, 0)),
        out_shape=jax.ShapeDtypeStruct((m, d), F32),
        compiler_params=_params("parallel"),
        name="out_ab",
    )(x, a, att, w)


def _mixer_c_kernel(x_ref, gn_ref, wb_ref, wc_ref, wh_ref, cw_ref, wo_ref, *rest, width, tiles_per_seq, rows_per_seq):
    short = rows_per_seq is not None
    if short:
        hist_ref, o_ref, u_ref, xn_ref, ubuf_ref = rest
    else:
        o_ref, tail_ref, xn_ref, ubuf_ref, carry_ref = rest
    i = pl.program_id(0)
    c = pl.program_id(1)
    tm = x_ref.shape[0]
    tc = wb_ref.shape[1]

    @pl.when(c == 0)
    def _():
        x = x_ref[...]
        xn_ref[...] = _rmsnorm(x, gn_ref[...]).astype(BF16)
        o_ref[...] = x
        if not short:
            @pl.when(i == 0)
            def _():
                carry_ref[...] = jnp.zeros(carry_ref.shape, F32)

    xn = xn_ref[...]
    gate_b = jnp.dot(xn, wb_ref[...], preferred_element_type=F32)
    u = jnp.dot(xn, wc_ref[...], preferred_element_type=F32) * jnp.dot(xn, wh_ref[...], preferred_element_type=F32)
    ubuf_ref[SUBLANES:SUBLANES + tm, :] = u
    if short:
        u_ref[...] = u
        ubuf_ref[0:SUBLANES, :] = jnp.zeros((SUBLANES, tc), F32)
    else:
        tail_ref[0] = u[tm - SUBLANES:tm, :]
        ubuf_ref[0:SUBLANES, :] = jnp.where(i % tiles_per_seq == 0, 0.0, carry_ref[c])
        carry_ref[c] = u[tm - SUBLANES:tm, :]
    acc = cw_ref[width - 1:width, :] * u
    for j in range(1, width):
        prev = ubuf_ref[SUBLANES - j:SUBLANES - j + tm, :]
        if short:
            pos = lax.broadcasted_iota(jnp.int32, (tm, 1), 0) & (rows_per_seq - 1)
            prev = jnp.where(pos >= j, prev, hist_ref[j - 1])
        acc = acc + cw_ref[width - 1 - j:width - j, :] * prev
    y = (gate_b * acc).astype(BF16)
    o_ref[...] += jnp.dot(y, wo_ref[...], preferred_element_type=F32)


def _mixer_c(x, gn, w_in, conv_w, w_out, batch, prefix):
    m, d = x.shape
    ch = w_out.shape[0]
    width = conv_w.shape[0]
    assert w_in.shape[1] == 3 * ch and width - 1 <= SUBLANES
    seq = m // batch
    tm = _row_tile(m, ROW_TILE)
    tc = _row_tile(ch, MIXC_TILE)
    nc = ch // tc
    short = seq < tm
    in_specs = [
        pl.BlockSpec((tm, d), lambda i, c: (i, 0)),
        pl.BlockSpec((1, d), lambda i, c: (0, 0)),
        pl.BlockSpec((d, tc), lambda i, c: (0, c)),
        pl.BlockSpec((d, tc), lambda i, c: (0, nc + c)),
        pl.BlockSpec((d, tc), lambda i, c: (0, 2 * nc + c)),
        pl.BlockSpec((width, tc), lambda i, c: (0, c)),
        pl.BlockSpec((tc, d), lambda i, c: (c, 0)),
    ]
    args = [x, gn.reshape(1, d), w_in, w_in, w_in, conv_w, w_out]
    o_spec = pl.BlockSpec((tm, d), lambda i, c: (i, 0))
    o_shape = jax.ShapeDtypeStruct((m, d), F32)
    scratch = [pltpu.VMEM((tm, d), BF16), pltpu.VMEM((SUBLANES + tm, tc), F32)]
    if short:
        assert tm % seq == 0 and seq & (seq - 1) == 0
        padded = jnp.concatenate([prefix, jnp.zeros((batch, seq, ch), F32)], axis=1)
        hist = jnp.stack([padded[:, width - 1 - j:width - 1 - j + seq].reshape(m, ch) for j in range(1, width)])
        in_specs.append(pl.BlockSpec((width - 1, tm, tc), lambda i, c: (0, i, c)))
        args.append(hist)
        out_specs = [o_spec, pl.BlockSpec((tm, tc), lambda i, c: (i, c))]
        out_shape = [o_shape, jax.ShapeDtypeStruct((m, ch), F32)]
        kern = functools.partial(_mixer_c_kernel, width=width, tiles_per_seq=None, rows_per_seq=seq)
    else:
        assert seq % tm == 0 and prefix is None
        out_specs = [o_spec, pl.BlockSpec((1, SUBLANES, tc), lambda i, c: (i, 0, c))]
        out_shape = [o_shape, jax.ShapeDtypeStruct((m // tm, SUBLANES, ch), F32)]
        scratch.append(pltpu.VMEM((nc, SUBLANES, tc), F32))
        kern = functools.partial(_mixer_c_kernel, width=width, tiles_per_seq=seq // tm, rows_per_seq=None)
    o, aux = pl.pallas_call(
        kern,
        grid=(m // tm, nc),
        in_specs=in_specs,
        out_specs=out_specs,
        out_shape=out_shape,
        scratch_shapes=scratch,
        compiler_params=_params("arbitrary", "arbitrary"),
        name="mixer_c",
    )(*args)
    if short:
        u_hist = jnp.concatenate([prefix, aux.reshape(batch, seq, ch)], axis=1)
        state = u_hist[:, -(width - 1):]
    else:
        tails = aux.reshape(batch, seq // tm, SUBLANES, ch)
        state = tails[:, -1, SUBLANES - (width - 1):, :]
    return o, state


def _forward(x3, w, conv_a_prefix, conv_c_prefix, attend):
    batch, seq, d = x3.shape
    m = batch * seq
    x = x3.reshape(m, d)
    seg = w["conv_a_w"].shape[1]
    hd = w["diff_subln_g"].shape[0]
    n_heads = seg // hd
    width_a = w["conv_a_w"].shape[0]

    def ffn(x, which, layer, final_norm=False):
        return _ffn(x, w["norm_" + which], w[which + "_w_gate"], w[which + "_w_up"], w[which + "_w_down"],
                    w["final_norm"], layer, final_norm=final_norm)

    x = ffn(x, "ffn1", 0)
    zf, zb = _proj_ab(x, w["norm_mix"][0], w["w_in_ab"], seg, (hd // 2) ** -0.5)
    conv_args = (w["conv_a_w"], w["conv_a_b"], w["conv_a_ln_g"], w["conv_a_ln_b"])
    if seq >= CONV_HALO:
        assert conv_a_prefix is None
        a, conv_a_state = _conv_a(zf, jnp.zeros((batch, width_a - 1, seg), F32), *conv_args, batch)
    else:
        tm = lambda t: jnp.swapaxes(t.reshape(batch, seq, seg), 0, 1)
        a, g = _conv_a_short(tm(zf[0]), tm(zf[1]), jnp.swapaxes(conv_a_prefix, 0, 1), *conv_args)
        a = jnp.swapaxes(a, 0, 1).reshape(m, seg)
        conv_a_state = jnp.concatenate([conv_a_prefix, jnp.swapaxes(g, 0, 1)], axis=1)[:, -(width_a - 1):]
    att = attend(zb, batch, seq, n_heads, hd)
    x = _out_ab(x, a, att, w["w_out_ab"])
    x = ffn(x, "ffn2", 0)

    x = ffn(x, "ffn1", 1)
    x, conv_c_state = _mixer_c(x, w["norm_mix"][1], w["w_in_c"], w["conv_c_w"], w["w_out_c"], batch, conv_c_prefix)
    y = ffn(x, "ffn2", 1, final_norm=True)
    return (y.reshape(batch, seq, d), zf[2].reshape(batch, seq, n_heads, hd), zf[3].reshape(batch, seq, n_heads, hd),
            conv_a_state, conv_c_state)


def kernel(x_prompt, x_sample, cache_k, cache_v, state_conv_a, state_conv_c, page_table, norm_ffn1, ffn1_w_gate, ffn1_w_up, ffn1_w_down, norm_mix, norm_ffn2, ffn2_w_gate, ffn2_w_up, ffn2_w_down, w_in_ab, conv_a_w, conv_a_b, conv_a_ln_g, conv_a_ln_b, diff_lambda, diff_subln_g, w_out_ab, w_in_c, conv_c_w, w_out_c, final_norm):
    w = dict(
        norm_ffn1=norm_ffn1, norm_mix=norm_mix, norm_ffn2=norm_ffn2, final_norm=final_norm,
        ffn1_w_gate=ffn1_w_gate.astype(BF16), ffn1_w_up=ffn1_w_up.astype(BF16), ffn1_w_down=ffn1_w_down.astype(BF16),
        ffn2_w_gate=ffn2_w_gate.astype(BF16), ffn2_w_up=ffn2_w_up.astype(BF16), ffn2_w_down=ffn2_w_down.astype(BF16),
        w_in_ab=w_in_ab.astype(BF16), w_out_ab=w_out_ab.astype(BF16),
        w_in_c=w_in_c.astype(BF16), w_out_c=w_out_c.astype(BF16),
        conv_a_w=conv_a_w, conv_a_b=conv_a_b, conv_a_ln_g=conv_a_ln_g, conv_a_ln_b=conv_a_ln_b,
        conv_c_w=conv_c_w, diff_subln_g=diff_subln_g,
    )

    def attend_prompt(qkv, batch, seq, n_heads, hd):
        return _attn_prompt(qkv, diff_lambda, diff_subln_g, batch, n_heads)

    def attend_sample(qkv, batch, seq, n_heads, hd):
        rows = lambda t: t.reshape(batch, seq * n_heads, hd)
        att = _attn_sample(rows(qkv[0]), rows(qkv[1]), rows(qkv[2]), cache_k, cache_v, page_table,
                           diff_lambda, diff_subln_g, n_heads)
        return att.reshape(batch * seq, n_heads * hd)

    y_p, k_p, v_p, ca_p, cc_p = _forward(x_prompt, w, None, None, attend_prompt)
    y_s, k_s, v_s, ca_s, cc_s = _forward(x_sample, w, state_conv_a, state_conv_c, attend_sample)
    return (y_p, y_s, k_p, v_p, ca_p, cc_p, k_s, v_s, ca_s, cc_s)
```

```python
import functools
import math

import jax
import jax.numpy as jnp
from jax import lax
from jax.experimental import pallas as pl
from jax.experimental.pallas import tpu as pltpu

F32 = jnp.float32
BF16 = jnp.bfloat16

RMS_EPS = 1e-6
LN_EPS = 1e-5
LAMBDA_INIT = 0.8 - 0.6 * math.exp(-0.3 * 0)
NEG_BIG = -1e30

V7X_VMEM_BYTES = 64 * 1024 * 1024
VMEM_LIMIT = V7X_VMEM_BYTES - 6 * 1024 * 1024
SUBLANES = 8
LANES = 128

ROW_TILE = 512
WIDE_ROW_TILE = 1024
FF_TILE = 512
MIXC_TILE = 512
CONV_TIME_TILE = 256
CONV_HALO = 32
ATTN_TILE = 512
PAGES_PER_STEP = 8


def _row_tile(m, target):
    t = min(m, target)
    while m % t or t % SUBLANES:
        t -= 1
    return t


def _params(*sem):
    return pltpu.CompilerParams(dimension_semantics=sem, vmem_limit_bytes=VMEM_LIMIT)


def _rmsnorm(x, g):
    return x * lax.rsqrt(jnp.mean(x * x, axis=-1, keepdims=True) + RMS_EPS) * g


def _silu(x):
    return x * jax.nn.sigmoid(x)


def _ffn_kernel(x_ref, gn_ref, wg_ref, wu_ref, wd_ref, fn_ref, o_ref, *rest, final_norm, emit_bf16):
    xn_ref = rest[-1]
    wgb_ref, wub_ref, wdb_ref = rest[:3] if emit_bf16 else (None, None, None)
    f = pl.program_id(1)

    @pl.when(f == 0)
    def _():
        x = x_ref[...]
        xn_ref[...] = _rmsnorm(x, gn_ref[...]).astype(BF16)
        o_ref[...] = x

    def weight(w_ref, wb_ref):
        if not emit_bf16:
            return w_ref[...]
        wb_ref[...] = w_ref[...].astype(BF16)
        return wb_ref[...]

    xn = xn_ref[...]
    g = jnp.dot(xn, weight(wg_ref, wgb_ref), preferred_element_type=F32)
    u = jnp.dot(xn, weight(wu_ref, wub_ref), preferred_element_type=F32)
    h = (_silu(g) * (0.5 * u)).astype(BF16)
    o_ref[...] += jnp.dot(h, weight(wd_ref, wdb_ref), preferred_element_type=F32)

    if final_norm:
        @pl.when(f == pl.num_programs(1) - 1)
        def _():
            o_ref[...] = _rmsnorm(o_ref[...], fn_ref[...])


def _ffn(x, gn, wg, wu, wd, fn, layer, *, final_norm):
    m, d = x.shape
    emit = wg.ndim == 3
    ff = wg.shape[-1]
    tm = _row_tile(m, WIDE_ROW_TILE)
    tf = _row_tile(ff, FF_TILE)
    if emit:
        assert m == tm
        w_specs = [
            pl.BlockSpec((None, d, tf), lambda i, f: (layer, 0, f)),
            pl.BlockSpec((None, d, tf), lambda i, f: (layer, 0, f)),
            pl.BlockSpec((None, tf, d), lambda i, f: (layer, f, 0)),
        ]
    else:
        w_specs = [
            pl.BlockSpec((d, tf), lambda i, f: (0, f)),
            pl.BlockSpec((d, tf), lambda i, f: (0, f)),
            pl.BlockSpec((tf, d), lambda i, f: (f, 0)),
        ]
    out_specs = [pl.BlockSpec((tm, d), lambda i, f: (i, 0))]
    out_shape = [jax.ShapeDtypeStruct((m, d), F32)]
    if emit:
        out_specs += [
            pl.BlockSpec((d, tf), lambda i, f: (0, f)),
            pl.BlockSpec((d, tf), lambda i, f: (0, f)),
            pl.BlockSpec((tf, d), lambda i, f: (f, 0)),
        ]
        out_shape += [jax.ShapeDtypeStruct((d, ff), BF16), jax.ShapeDtypeStruct((d, ff), BF16),
                      jax.ShapeDtypeStruct((ff, d), BF16)]
    outs = pl.pallas_call(
        functools.partial(_ffn_kernel, final_norm=final_norm, emit_bf16=emit),
        grid=(m // tm, ff // tf),
        in_specs=[
            pl.BlockSpec((tm, d), lambda i, f: (i, 0)),
            pl.BlockSpec((None, 1, d), lambda i, f: (layer, 0, 0)),
            *w_specs,
            pl.BlockSpec((1, d), lambda i, f: (0, 0)),
        ],
        out_specs=out_specs,
        out_shape=out_shape,
        scratch_shapes=[pltpu.VMEM((tm, d), BF16)],
        compiler_params=_params("parallel", "arbitrary"),
        name="ffn",
    )(x, gn.reshape(gn.shape[0], 1, d), wg, wu, wd, fn.reshape(1, d))
    return (outs[0], tuple(outs[1:])) if emit else (outs[0], (wg, wu, wd))


def _proj_ab_kernel(x_ref, gn_ref, w_ref, zf_ref, zb_ref, *rest, q_scale, emit_bf16):
    wb_ref, xn_ref = rest if emit_bf16 else (None, rest[0])
    j = pl.program_id(1)

    @pl.when(j == 0)
    def _():
        xn_ref[...] = _rmsnorm(x_ref[...], gn_ref[...]).astype(BF16)

    w = w_ref[...]
    if wb_ref is not None:
        w = w.astype(BF16)
        wb_ref[...] = w
    z = jnp.dot(xn_ref[...], w, preferred_element_type=F32)
    zf_ref[...] = z
    zb_ref[...] = (z * jnp.where(j == 0, q_scale, 1.0)).astype(BF16)


def _proj_ab(x, gn, w, seg, q_scale):
    m, d = x.shape
    assert w.shape[1] == 5 * seg
    emit = w.dtype != BF16
    tm = _row_tile(m, WIDE_ROW_TILE)
    assert not emit or m == tm
    w_col = lambda j: jnp.where(j == 0, 2, jnp.where(j <= 2, j - 1, j))
    zb_blk = lambda j: jnp.where(j == 0, 0, jnp.where(j <= 3, 1, 2))
    w_spec = pl.BlockSpec((d, seg), lambda i, j: (0, w_col(j)))
    out_specs = [
        pl.BlockSpec((None, tm, seg), lambda i, j: (jnp.maximum(j - 1, 0), i, 0)),
        pl.BlockSpec((None, tm, seg), lambda i, j: (zb_blk(j), i, 0)),
    ]
    out_shape = [jax.ShapeDtypeStruct((4, m, seg), F32), jax.ShapeDtypeStruct((3, m, seg), BF16)]
    if emit:
        out_specs.append(w_spec)
        out_shape.append(jax.ShapeDtypeStruct(w.shape, BF16))
    outs = pl.pallas_call(
        functools.partial(_proj_ab_kernel, q_scale=q_scale, emit_bf16=emit),
        grid=(m // tm, 5),
        in_specs=[
            pl.BlockSpec((tm, d), lambda i, j: (i, 0)),
            pl.BlockSpec((1, d), lambda i, j: (0, 0)),
            w_spec,
        ],
        out_specs=out_specs,
        out_shape=out_shape,
        scratch_shapes=[pltpu.VMEM((tm, d), BF16)],
        compiler_params=_params("parallel", "arbitrary"),
        name="proj_ab",
    )(x, gn.reshape(1, d), w)
    return outs[0], outs[1], (outs[2] if emit else w)


def _ln_silu(a, lg, lb):
    mu = jnp.mean(a, axis=-1, keepdims=True)
    c = a - mu
    var = jnp.mean(c * c, axis=-1, keepdims=True)
    return _silu(c * lax.rsqrt(var + LN_EPS) * lg + lb)


def _conv_a_kernel(val_ref, gate_ref, pre_ref, w_ref, b_ref, lg_ref, lb_ref, a_ref, tail_ref,
                   buf_ref, sh_ref, c_ref, *, width):
    t = pl.program_id(1)
    tt, ch = val_ref.shape

    @pl.when(t == 0)
    def _():
        buf_ref[0:CONV_HALO, :] = pre_ref[0]

    @pl.when(t > 0)
    def _():
        buf_ref[0:CONV_HALO, :] = buf_ref[tt:tt + CONV_HALO, :]

    buf_ref[CONV_HALO:CONV_HALO + tt, :] = val_ref[...] * jax.nn.sigmoid(gate_ref[...])
    tail_ref[0] = buf_ref[tt:tt + CONV_HALO, :]
    n_sh = sh_ref.shape[1]
    for s in range(1, SUBLANES):
        sh_ref[s - 1] = buf_ref[s:s + n_sh, :]
    off = CONV_HALO - (width - 1)
    cc, rr = 2 * LANES, 8 * SUBLANES
    for c0 in range(0, ch, cc):
        for r0 in range(0, tt, rr):
            acc = jnp.zeros((rr, cc), F32)
            for j in range(width):
                s = (off + j) % SUBLANES
                a0 = off + j - s + r0
                if s == 0:
                    win = buf_ref[a0:a0 + rr, c0:c0 + cc]
                else:
                    win = sh_ref[s - 1, a0:a0 + rr, c0:c0 + cc]
                acc = acc + w_ref[j:j + 1, c0:c0 + cc] * win
            c_ref[r0:r0 + rr, c0:c0 + cc] = acc + b_ref[:, c0:c0 + cc]
    ln_rows = 8 * SUBLANES
    for r0 in range(0, tt, ln_rows):
        y = _ln_silu(c_ref[r0:r0 + ln_rows, :], lg_ref[...], lb_ref[...])
        a_ref[r0:r0 + ln_rows, :] = y.astype(BF16)


def _conv_a(zf, prefix, w, b, lg, lb, batch):
    _, m, ch = zf.shape
    seq = m // batch
    width = w.shape[0]
    assert width - 1 <= CONV_HALO
    tt = _row_tile(seq, CONV_TIME_TILE)
    assert tt >= CONV_HALO and tt % (8 * SUBLANES) == 0
    nt = seq // tt
    pre = jnp.pad(prefix, ((0, 0), (CONV_HALO - (width - 1), 0), (0, 0)))
    vec = lambda: pl.BlockSpec((1, ch), lambda bi, t: (0, 0))
    a, tail = pl.pallas_call(
        functools.partial(_conv_a_kernel, width=width),
        grid=(batch, nt),
        in_specs=[
            pl.BlockSpec((None, tt, ch), lambda bi, t: (0, bi * nt + t, 0)),
            pl.BlockSpec((None, tt, ch), lambda bi, t: (1, bi * nt + t, 0)),
            pl.BlockSpec((1, CONV_HALO, ch), lambda bi, t: (bi, 0, 0)),
            pl.BlockSpec((width, ch), lambda bi, t: (0, 0)),
            vec(), vec(), vec(),
        ],
        out_specs=[
            pl.BlockSpec((tt, ch), lambda bi, t: (bi * nt + t, 0)),
            pl.BlockSpec((1, CONV_HALO, ch), lambda bi, t: (bi, 0, 0)),
        ],
        out_shape=[jax.ShapeDtypeStruct((m, ch), BF16), jax.ShapeDtypeStruct((batch, CONV_HALO, ch), F32)],
        scratch_shapes=[
            pltpu.VMEM((CONV_HALO + tt, ch), F32),
            pltpu.VMEM((SUBLANES - 1, CONV_HALO + tt - SUBLANES, ch), F32),
            pltpu.VMEM((tt, ch), F32),
        ],
        compiler_params=_params("parallel", "arbitrary"),
        name="conv_a",
    )(zf, zf, pre, w, b.reshape(1, ch), lg.reshape(1, ch), lb.reshape(1, ch))
    return a, tail[:, CONV_HALO - (width - 1):]


def _conv_a_short_kernel(val_ref, gate_ref, pre_ref, w_ref, b_ref, lg_ref, lb_ref, a_ref, g_ref, *, width):
    steps = a_ref.shape[0]
    g_ref[...] = val_ref[...] * jax.nn.sigmoid(gate_ref[...])
    for t in range(steps):
        acc = jnp.zeros(a_ref.shape[1:], F32)
        for j in range(width):
            i = t + j - (width - 1)
            acc = acc + w_ref[j:j + 1, :] * (g_ref[i] if i >= 0 else pre_ref[i + width - 1])
        a_ref[t] = _ln_silu(acc + b_ref[...], lg_ref[...], lb_ref[...]).astype(BF16)


def _conv_a_short(val, gate, pre, w, b, lg, lb):
    steps, batch, ch = val.shape
    width = w.shape[0]
    assert pre.shape[0] == width - 1
    return pl.pallas_call(
        functools.partial(_conv_a_short_kernel, width=width),
        out_shape=[jax.ShapeDtypeStruct((steps, batch, ch), BF16), jax.ShapeDtypeStruct((steps, batch, ch), F32)],
        compiler_params=pltpu.CompilerParams(vmem_limit_bytes=VMEM_LIMIT),
        name="conv_a_short",
    )(val, gate, pre, w, b.reshape(1, ch), lg.reshape(1, ch), lb.reshape(1, ch))


def _lambda(lp):
    e1 = jnp.exp(jnp.sum(lp[0:1] * lp[1:2], axis=-1, keepdims=True))
    e2 = jnp.exp(jnp.sum(lp[2:3] * lp[3:4], axis=-1, keepdims=True))
    return e1 - e2 + LAMBDA_INIT


def _head_norm(o, g):
    return o * lax.rsqrt(jnp.mean(o * o, axis=-1, keepdims=True) + RMS_EPS) * g * (1.0 - LAMBDA_INIT)


_NT = (((1,), (1,)), ((), ()))
_TN = (((0,), (0,)), ((), ()))


def _attn_prompt_kernel(lam_ref, g_ref, q_ref, k_ref, v_ref, o_ref, m_ref, l_ref, acc_ref, *, n_heads):
    qi = pl.program_id(1)
    ki = pl.program_id(2)
    tq = q_ref.shape[0]
    tk = k_ref.shape[0]
    hd = q_ref.shape[1] // n_heads

    @pl.when(ki == 0)
    def _():
        m_ref[...] = jnp.full(m_ref.shape, NEG_BIG, F32)
        l_ref[...] = jnp.zeros(l_ref.shape, F32)
        acc_ref[...] = jnp.zeros(acc_ref.shape, F32)

    def step(diagonal):
        rel = (lax.broadcasted_iota(jnp.int32, (tk, tq), 0) - lax.broadcasted_iota(jnp.int32, (tk, tq), 1)
               + (ki * tk - qi * tq)).astype(F32)
        lane = lax.broadcasted_iota(jnp.int32, (tq, hd), 1)
        for h in range(n_heads):
            cols = slice(h * hd, (h + 1) * hd)
            qh = q_ref[:, cols]
            kh = k_ref[:, cols]
            vh = v_ref[:, cols]
            bias = (2.0 ** -(8.0 * (h + 1) / n_heads)) * rel
            if diagonal:
                bias = jnp.where(rel <= 0, bias, NEG_BIG)
            for mp in range(2):
                half = (lane < hd // 2) if mp == 0 else (lane >= hd // 2)
                qm = jnp.where(half, qh, jnp.zeros_like(qh))
                s = lax.dot_general(kh, qm, _NT, preferred_element_type=F32) + bias
                idx = 2 * h + mp
                m_prev = m_ref[idx]
                m_new = jnp.maximum(m_prev, jnp.max(s, axis=0, keepdims=True))
                alpha = jnp.exp(m_prev - m_new)
                p = jnp.exp(s - m_new)
                l_ref[idx] = alpha * l_ref[idx] + jnp.sum(p, axis=0, keepdims=True)
                pv = lax.dot_general(vh, p.astype(BF16), _TN, preferred_element_type=F32)
                acc_ref[idx] = alpha * acc_ref[idx] + pv
                m_ref[idx] = m_new

    @pl.when(ki < qi)
    def _():
        step(False)

    @pl.when(ki == qi)
    def _():
        step(True)
        lam = _lambda(lam_ref[...])
        for h in range(n_heads):
            o = (acc_ref[2 * h] * (1.0 / l_ref[2 * h])
                 - lam * (acc_ref[2 * h + 1] * (1.0 / l_ref[2 * h + 1])))
            y = o * lax.rsqrt(jnp.mean(o * o, axis=0, keepdims=True) + RMS_EPS) * g_ref[...] * (1.0 - LAMBDA_INIT)
            o_ref[:, h * hd:(h + 1) * hd] = y.T.astype(BF16)


def _attn_prompt(qkv, lam_params, subln_g, batch, n_heads):
    _, m, ad = qkv.shape
    seq = m // batch
    hd = ad // n_heads
    tq = _row_tile(seq, ATTN_TILE)
    nq = seq // tq
    kv_spec = lambda which: pl.BlockSpec(
        (None, tq, ad), lambda b, qi, ki: (which, b * nq + jnp.minimum(ki, qi), 0))
    return pl.pallas_call(
        functools.partial(_attn_prompt_kernel, n_heads=n_heads),
        grid=(batch, nq, nq),
        in_specs=[
            pl.BlockSpec(lam_params.shape, lambda b, qi, ki: (0, 0)),
            pl.BlockSpec((hd, 1), lambda b, qi, ki: (0, 0)),
            pl.BlockSpec((None, tq, ad), lambda b, qi, ki: (0, b * nq + qi, 0)),
            kv_spec(1), kv_spec(2),
        ],
        out_specs=pl.BlockSpec((tq, ad), lambda b, qi, ki: (b * nq + qi, 0)),
        out_shape=jax.ShapeDtypeStruct((m, ad), BF16),
        scratch_shapes=[
            pltpu.VMEM((2 * n_heads, 1, tq), F32),
            pltpu.VMEM((2 * n_heads, 1, tq), F32),
            pltpu.VMEM((2 * n_heads, hd, tq), F32),
        ],
        compiler_params=_params("parallel", "parallel", "arbitrary"),
        name="attn_prompt",
    )(lam_params, subln_g.reshape(hd, 1), qkv, qkv, qkv)


def _attn_sample_kernel(pt_ref, lam_ref, g_ref, q_ref, kn_ref, vn_ref, *rest, pp, past, n_heads):
    k_pages = rest[:pp]
    v_pages = rest[pp:2 * pp]
    o_ref, m_ref, l_ref, acc_ref, qm_ref, bias_ref = rest[2 * pp:]
    st = pl.program_id(1)
    r = q_ref.shape[1]
    hd = q_ref.shape[2]
    page_tokens = k_pages[0].shape[1]
    pc = page_tokens * n_heads
    hshift = n_heads.bit_length() - 1
    assert 1 << hshift == n_heads
    row = lax.broadcasted_iota(jnp.int32, (2 * r, 1), 0)
    row_head = row & (n_heads - 1)
    row_step = (row & (r - 1)) >> hshift
    slope = jnp.zeros((2 * r, 1), F32)
    for h in range(n_heads):
        slope = jnp.where(row_head == h, 2.0 ** -(8.0 * (h + 1) / n_heads), slope)

    @pl.when(st == 0)
    def _():
        m_ref[...] = jnp.full(m_ref.shape, NEG_BIG, F32)
        l_ref[...] = jnp.zeros(l_ref.shape, F32)
        acc_ref[...] = jnp.zeros(acc_ref.shape, F32)
        q = q_ref[0]
        lane = lax.broadcasted_iota(jnp.int32, q.shape, 1)
        zero = jnp.zeros_like(q)
        qm_ref[...] = jnp.concatenate(
            [jnp.where(lane < hd // 2, q, zero), jnp.where(lane >= hd // 2, q, zero)], axis=0)
        col = lax.broadcasted_iota(jnp.int32, (1, pc), 1)
        rel0 = ((col >> hshift) - row_step - past).astype(F32)
        bias_ref[...] = jnp.where((col & (n_heads - 1)) == row_head, slope * rel0, NEG_BIG)

    qm = qm_ref[...]

    def update(tiles):
        m_prev = m_ref[...]
        m_cur = None
        for s, c, _ in tiles:
            m_t = jnp.max(s, axis=-1, keepdims=True) + c
            m_cur = m_t if m_cur is None else jnp.maximum(m_cur, m_t)
        m_new = jnp.maximum(m_prev, m_cur)
        alpha = jnp.exp(m_prev - m_new)
        lsum = alpha * l_ref[...]
        acc = alpha * acc_ref[...]
        for s, c, v in tiles:
            p = jnp.exp(s - (m_new[:, :1] - c))
            lsum = lsum + jnp.sum(p, axis=-1, keepdims=True)
            acc = acc + jnp.dot(p.astype(BF16), v, preferred_element_type=F32)
        l_ref[...] = lsum
        acc_ref[...] = acc
        m_ref[...] = m_new

    tiles = []
    for p in range(pp):
        kp = k_pages[p][0].reshape(pc, hd).astype(BF16)
        base = ((st * pp + p) * page_tokens).astype(F32)
        s = lax.dot_general(qm, kp, _NT, preferred_element_type=F32) + bias_ref[...]
        tiles.append((s, slope * base, v_pages[p][0].reshape(pc, hd).astype(BF16)))
    update(tiles)

    @pl.when(st == pl.num_programs(1) - 1)
    def _():
        ncol = lax.broadcasted_iota(jnp.int32, (1, r), 1)
        dist = ((ncol >> hshift) - row_step)
        ok = ((ncol & (n_heads - 1)) == row_head) & (dist <= 0)
        s = lax.dot_general(qm, kn_ref[0], _NT, preferred_element_type=F32) + slope * dist.astype(F32)
        update([(jnp.where(ok, s, NEG_BIG), jnp.zeros((2 * r, 1), F32), vn_ref[0])])
        lam = _lambda(lam_ref[...])
        o = acc_ref[0:r] / l_ref[0:r] - lam * (acc_ref[r:2 * r] / l_ref[r:2 * r])
        o_ref[0] = _head_norm(o, g_ref[...]).astype(BF16)


def _attn_sample(q, kn, vn, cache_k, cache_v, page_table, lam_params, subln_g, n_heads):
    batch, r, hd = q.shape
    n_pages = page_table.shape[1]
    page_tokens = cache_k.shape[1]
    pp = PAGES_PER_STEP
    while n_pages % pp:
        pp -= 1
    assert r & (r - 1) == 0
    page_specs = [
        pl.BlockSpec((1,) + cache_k.shape[1:], functools.partial(lambda b, s, pt, p: (pt[b, s * pp + p], 0, 0, 0), p=p))
        for p in range(pp)
    ]
    new_spec = pl.BlockSpec((1, r, hd), lambda b, s, pt: (b, 0, 0))
    grid_spec = pltpu.PrefetchScalarGridSpec(
        num_scalar_prefetch=1,
        grid=(batch, n_pages // pp),
        in_specs=[
            pl.BlockSpec(lam_params.shape, lambda b, s, pt: (0, 0)),
            pl.BlockSpec((1, hd), lambda b, s, pt: (0, 0)),
            new_spec, new_spec, new_spec,
        ] + page_specs + page_specs,
        out_specs=new_spec,
        scratch_shapes=[pltpu.VMEM((2 * r, hd), F32)] * 3 + [
            pltpu.VMEM((2 * r, hd), BF16),
            pltpu.VMEM((2 * r, page_tokens * n_heads), F32),
        ],
    )
    return pl.pallas_call(
        functools.partial(_attn_sample_kernel, pp=pp, past=n_pages * page_tokens, n_heads=n_heads),
        grid_spec=grid_spec,
        out_shape=jax.ShapeDtypeStruct((batch, r, hd), BF16),
        compiler_params=_params("parallel", "arbitrary"),
        name="attn_sample",
    )(page_table, lam_params, subln_g.reshape(1, hd), q, kn, vn,
      *([cache_k] * pp), *([cache_v] * pp))


def _out_ab_kernel(x_ref, a_ref, att_ref, w_ref, o_ref):
    ca = a_ref.shape[1]
    o_ref[...] = (x_ref[...]
                  + jnp.dot(a_ref[...], w_ref[0:ca, :], preferred_element_type=F32)
                  + jnp.dot(att_ref[...], w_ref[ca:, :], preferred_element_type=F32))


def _out_ab_cast_kernel(x_ref, aa_ref, w_ref, o_ref, wb_ref):
    w = w_ref[...].astype(BF16)
    wb_ref[...] = w

    @pl.when(pl.program_id(0) == 0)
    def _():
        o_ref[...] = x_ref[...]

    o_ref[...] += jnp.dot(aa_ref[...], w, preferred_element_type=F32)


def _out_ab_cast(x, a, att, w):
    m, d = x.shape
    aa = jnp.concatenate([a, att], axis=1)
    kc = _row_tile(w.shape[0], ROW_TILE)
    return pl.pallas_call(
        _out_ab_cast_kernel,
        grid=(w.shape[0] // kc,),
        in_specs=[
            pl.BlockSpec((m, d), lambda k: (0, 0)),
            pl.BlockSpec((m, kc), lambda k: (0, k)),
            pl.BlockSpec((kc, d), lambda k: (k, 0)),
        ],
        out_specs=[pl.BlockSpec((m, d), lambda k: (0, 0)), pl.BlockSpec((kc, d), lambda k: (k, 0))],
        out_shape=[jax.ShapeDtypeStruct((m, d), F32), jax.ShapeDtypeStruct(w.shape, BF16)],
        compiler_params=_params("arbitrary"),
        name="out_ab_cast",
    )(x, aa, w)


def _out_ab(x, a, att, w):
    m, d = x.shape
    tm = _row_tile(m, ROW_TILE)
    return pl.pallas_call(
        _out_ab_kernel,
        grid=(m // tm,),
        in_specs=[
            pl.BlockSpec((tm, d), lambda i: (i, 0)),
            pl.BlockSpec((tm, a.shape[1]), lambda i: (i, 0)),
            pl.BlockSpec((tm, att.shape[1]), lambda i: (i, 0)),
            pl.BlockSpec(w.shape, lambda i: (0, 0)),
        ],
        out_specs=pl.BlockSpec((tm, d), lambda i: (i, 0)),
        out_shape=jax.ShapeDtypeStruct((m, d), F32),
        compiler_params=_params("parallel"),
        name="out_ab",
    )(x, a, att, w)


def _mixer_c_kernel(x_ref, gn_ref, wb_ref, wc_ref, wh_ref, cw_ref, wo_ref, *rest, width, tiles_per_seq, rows_per_seq):
    short = rows_per_seq is not None
    if short:
        hist_ref, o_ref, u_ref, wbb_ref, wcb_ref, whb_ref, wob_ref, xn_ref, ubuf_ref = rest
    else:
        o_ref, tail_ref, xn_ref, ubuf_ref, carry_ref = rest
        wbb_ref = wcb_ref = whb_ref = wob_ref = None

    def weight(w_ref, w16_ref):
        if not short:
            return w_ref[...]
        w16_ref[...] = w_ref[...].astype(BF16)
        return w16_ref[...]

    i = pl.program_id(0)
    c = pl.program_id(1)
    tm = x_ref.shape[0]
    tc = wb_ref.shape[1]

    @pl.when(c == 0)
    def _():
        x = x_ref[...]
        xn_ref[...] = _rmsnorm(x, gn_ref[...]).astype(BF16)
        o_ref[...] = x
        if not short:
            @pl.when(i == 0)
            def _():
                carry_ref[...] = jnp.zeros(carry_ref.shape, F32)

    xn = xn_ref[...]
    gate_b = jnp.dot(xn, weight(wb_ref, wbb_ref), preferred_element_type=F32)
    u = (jnp.dot(xn, weight(wc_ref, wcb_ref), preferred_element_type=F32)
         * jnp.dot(xn, weight(wh_ref, whb_ref), preferred_element_type=F32))
    ubuf_ref[SUBLANES:SUBLANES + tm, :] = u
    if short:
        u_ref[...] = u
        ubuf_ref[0:SUBLANES, :] = jnp.zeros((SUBLANES, tc), F32)
    else:
        tail_ref[0] = u[tm - SUBLANES:tm, :]
        ubuf_ref[0:SUBLANES, :] = jnp.where(i % tiles_per_seq == 0, 0.0, carry_ref[c])
        carry_ref[c] = u[tm - SUBLANES:tm, :]
    acc = cw_ref[width - 1:width, :] * u
    for j in range(1, width):
        prev = ubuf_ref[SUBLANES - j:SUBLANES - j + tm, :]
        if short:
            pos = lax.broadcasted_iota(jnp.int32, (tm, 1), 0) & (rows_per_seq - 1)
            prev = jnp.where(pos >= j, prev, hist_ref[j - 1])
        acc = acc + cw_ref[width - 1 - j:width - j, :] * prev
    y = (gate_b * acc).astype(BF16)
    o_ref[...] += jnp.dot(y, weight(wo_ref, wob_ref), preferred_element_type=F32)


def _mixer_c(x, gn, w_in, conv_w, w_out, batch, prefix):
    m, d = x.shape
    ch = w_out.shape[0]
    width = conv_w.shape[0]
    assert width - 1 <= SUBLANES
    seq = m // batch
    tm = _row_tile(m, ROW_TILE)
    short = seq < tm
    tc = _row_tile(ch, MIXC_TILE // 2 if short else MIXC_TILE)
    nc = ch // tc
    col = lambda part: pl.BlockSpec((d, tc), lambda i, c: (0, part * nc + c))
    wo_spec = pl.BlockSpec((tc, d), lambda i, c: (c, 0))
    if short:
        assert w_in.shape[1] == 3 * ch
        w_specs, w_args = [col(0), col(1), col(2)], [w_in, w_in, w_in]
    else:
        w_specs, w_args = [col(0)] * 3, list(w_in)
    in_specs = [
        pl.BlockSpec((tm, d), lambda i, c: (i, 0)),
        pl.BlockSpec((1, d), lambda i, c: (0, 0)),
        *w_specs,
        pl.BlockSpec((width, tc), lambda i, c: (0, c)),
        wo_spec,
    ]
    args = [x, gn.reshape(1, d), *w_args, conv_w, w_out]
    o_spec = pl.BlockSpec((tm, d), lambda i, c: (i, 0))
    o_shape = jax.ShapeDtypeStruct((m, d), F32)
    scratch = [pltpu.VMEM((tm, d), BF16), pltpu.VMEM((SUBLANES + tm, tc), F32)]
    if short:
        assert m == tm and tm % seq == 0 and seq & (seq - 1) == 0
        padded = jnp.concatenate([prefix, jnp.zeros((batch, seq, ch), F32)], axis=1)
        hist = jnp.stack([padded[:, width - 1 - j:width - 1 - j + seq].reshape(m, ch) for j in range(1, width)])
        in_specs.append(pl.BlockSpec((width - 1, tm, tc), lambda i, c: (0, i, c)))
        args.append(hist)
        out_specs = [o_spec, pl.BlockSpec((tm, tc), lambda i, c: (i, c)), col(0), col(0), col(0), wo_spec]
        out_shape = [o_shape, jax.ShapeDtypeStruct((m, ch), F32)] + [jax.ShapeDtypeStruct((d, ch), BF16)] * 3 + [
            jax.ShapeDtypeStruct((ch, d), BF16)]
        kern = functools.partial(_mixer_c_kernel, width=width, tiles_per_seq=None, rows_per_seq=seq)
    else:
        assert seq % tm == 0 and prefix is None
        out_specs = [o_spec, pl.BlockSpec((1, SUBLANES, tc), lambda i, c: (i, 0, c))]
        out_shape = [o_shape, jax.ShapeDtypeStruct((m // tm, SUBLANES, ch), F32)]
        scratch.append(pltpu.VMEM((nc, SUBLANES, tc), F32))
        kern = functools.partial(_mixer_c_kernel, width=width, tiles_per_seq=seq // tm, rows_per_seq=None)
    o, aux, *wb16 = pl.pallas_call(
        kern,
        grid=(m // tm, nc),
        in_specs=in_specs,
        out_specs=out_specs,
        out_shape=out_shape,
        scratch_shapes=scratch,
        compiler_params=_params("arbitrary", "arbitrary"),
        name="mixer_c",
    )(*args)
    if short:
        u_hist = jnp.concatenate([prefix, aux.reshape(batch, seq, ch)], axis=1)
        state = u_hist[:, -(width - 1):]
        return o, state, (tuple(wb16[:3]), wb16[3])
    tails = aux.reshape(batch, seq // tm, SUBLANES, ch)
    state = tails[:, -1, SUBLANES - (width - 1):, :]
    return o, state, (w_in, w_out)


def _forward(x3, w, mm, conv_a_prefix, conv_c_prefix, attend):
    batch, seq, d = x3.shape
    m = batch * seq
    x = x3.reshape(m, d)
    seg = w["conv_a_w"].shape[1]
    hd = w["diff_subln_g"].shape[0]
    n_heads = seg // hd
    width_a = w["conv_a_w"].shape[0]
    mm16 = {}

    def ffn(x, which, layer, final_norm=False):
        y, mm16[which, layer] = _ffn(x, w["norm_" + which], *mm[which, layer], w["final_norm"], layer,
                                     final_norm=final_norm)
        return y

    x = ffn(x, "ffn1", 0)
    zf, zb, mm16["w_in_ab"] = _proj_ab(x, w["norm_mix"][0], mm["w_in_ab"], seg, (hd // 2) ** -0.5)
    conv_args = (w["conv_a_w"], w["conv_a_b"], w["conv_a_ln_g"], w["conv_a_ln_b"])
    if seq >= CONV_HALO:
        assert conv_a_prefix is None
        a, conv_a_state = _conv_a(zf, jnp.zeros((batch, width_a - 1, seg), F32), *conv_args, batch)
    else:
        tm = lambda t: jnp.swapaxes(t.reshape(batch, seq, seg), 0, 1)
        a, g = _conv_a_short(tm(zf[0]), tm(zf[1]), jnp.swapaxes(conv_a_prefix, 0, 1), *conv_args)
        a = jnp.swapaxes(a, 0, 1).reshape(m, seg)
        conv_a_state = jnp.concatenate([conv_a_prefix, jnp.swapaxes(g, 0, 1)], axis=1)[:, -(width_a - 1):]
    att = attend(zb, batch, seq, n_heads, hd)
    if mm["w_out_ab"].dtype == BF16:
        x, mm16["w_out_ab"] = _out_ab(x, a, att, mm["w_out_ab"]), mm["w_out_ab"]
    else:
        x, mm16["w_out_ab"] = _out_ab_cast(x, a, att, mm["w_out_ab"])
    x = ffn(x, "ffn2", 0)

    x = ffn(x, "ffn1", 1)
    x, conv_c_state, (mm16["w_in_c"], mm16["w_out_c"]) = _mixer_c(
        x, w["norm_mix"][1], mm["w_in_c"], w["conv_c_w"], mm["w_out_c"], batch, conv_c_prefix)
    y = ffn(x, "ffn2", 1, final_norm=True)
    outs = (y.reshape(batch, seq, d), zf[2].reshape(batch, seq, n_heads, hd), zf[3].reshape(batch, seq, n_heads, hd),
            conv_a_state, conv_c_state)
    return outs, mm16


def kernel(x_prompt, x_sample, cache_k, cache_v, state_conv_a, state_conv_c, page_table, norm_ffn1, ffn1_w_gate, ffn1_w_up, ffn1_w_down, norm_mix, norm_ffn2, ffn2_w_gate, ffn2_w_up, ffn2_w_down, w_in_ab, conv_a_w, conv_a_b, conv_a_ln_g, conv_a_ln_b, diff_lambda, diff_subln_g, w_out_ab, w_in_c, conv_c_w, w_out_c, final_norm):
    w = dict(
        norm_ffn1=norm_ffn1, norm_mix=norm_mix, norm_ffn2=norm_ffn2, final_norm=final_norm,
        conv_a_w=conv_a_w, conv_a_b=conv_a_b, conv_a_ln_g=conv_a_ln_g, conv_a_ln_b=conv_a_ln_b,
        conv_c_w=conv_c_w, diff_subln_g=diff_subln_g,
    )
    depth = norm_ffn1.shape[0]
    mm = dict(w_in_ab=w_in_ab, w_out_ab=w_out_ab, w_in_c=w_in_c, w_out_c=w_out_c)
    for layer in range(depth):
        mm["ffn1", layer] = (ffn1_w_gate, ffn1_w_up, ffn1_w_down)
        mm["ffn2", layer] = (ffn2_w_gate, ffn2_w_up, ffn2_w_down)

    def attend_prompt(qkv, batch, seq, n_heads, hd):
        return _attn_prompt(qkv, diff_lambda, diff_subln_g, batch, n_heads)

    def attend_sample(qkv, batch, seq, n_heads, hd):
        rows = lambda t: t.reshape(batch, seq * n_heads, hd)
        att = _attn_sample(rows(qkv[0]), rows(qkv[1]), rows(qkv[2]), cache_k, cache_v, page_table,
                           diff_lambda, diff_subln_g, n_heads)
        return att.reshape(batch * seq, n_heads * hd)

    (y_s, k_s, v_s, ca_s, cc_s), mm16 = _forward(x_sample, w, mm, state_conv_a, state_conv_c, attend_sample)
    (y_p, k_p, v_p, ca_p, cc_p), _ = _forward(x_prompt, w, mm16, None, None, attend_prompt)
    return (y_p, y_s, k_p, v_p, ca_p, cc_p, k_s, v_s, ca_s, cc_s)
```
